```python
import jax, jax.numpy as jnp
from jax import lax
import numpy as np

D_MODEL = 2048
BATCH = 2
SEQ = 8192
DEPTH = 4

MEM_LEN = 256
EPS = 1e-6
NEG = -1e30
ATT_HEADS = 8
ATT_HEAD_DIM = 128
ATT_WIDTH = ATT_HEADS * ATT_HEAD_DIM
MOBA_BLOCK = 256
MOBA_TOPK = 3
MOBA_QCHUNK = 64
ROPE_THETA = 10000.0
SSD_WIDTH = D_MODEL
SSD_HEAD_DIM = 64
SSD_HEADS = SSD_WIDTH // SSD_HEAD_DIM
SSD_GROUPS = 4
SSD_STATE = 128
SSD_CONV = 4
SSD_CHUNK = 256
SSD_CONV_DIM = SSD_WIDTH + 2 * SSD_GROUPS * SSD_STATE
MEM_HEADS = 4
MEM_HEAD_DIM = 256
MEM_WIDTH = MEM_HEADS * MEM_HEAD_DIM
D_FF = 5632
FFN_CONV = 3
IN_SIZES = (ATT_WIDTH, ATT_WIDTH, ATT_WIDTH, SSD_WIDTH, SSD_CONV_DIM, SSD_HEADS, MEM_WIDTH, D_MODEL, D_MODEL, D_MODEL)
N_IN = 3 * ATT_WIDTH + SSD_WIDTH + SSD_CONV_DIM + SSD_HEADS + MEM_WIDTH + 3 * D_MODEL

kernel_name = 'hybrid_moba_ssd_memory_gated_block'


def _split(t, sizes):
    idx = []
    acc = 0
    for s in sizes[:-1]:
        acc += s
        idx.append(acc)
    return jnp.split(t, idx, axis=-1)


def rmsnorm(x, w):
    xf = x.astype(jnp.float32)
    y = xf * lax.rsqrt(jnp.mean(xf * xf, axis=-1, keepdims=True) + EPS) * w.astype(jnp.float32)
    return y.astype(x.dtype)


def group_rmsnorm(y, w, groups):
    shp = y.shape
    yf = y.astype(jnp.float32).reshape(shp[:-1] + (groups, shp[-1] // groups))
    yf = yf * lax.rsqrt(jnp.mean(yf * yf, axis=-1, keepdims=True) + EPS)
    return yf.reshape(shp) * w.astype(jnp.float32)


def causal_dwconv(x, w, b):
    W, C = w.shape
    y = lax.conv_general_dilated(x, w[:, None, :].astype(x.dtype), window_strides=(1,),
                                 padding=[(W - 1, 0)], dimension_numbers=('NWC', 'WIO', 'NWC'),
                                 feature_group_count=C)
    return y + b.astype(x.dtype)


def rope(t, pos):
    half = t.shape[-1] // 2
    inv = ROPE_THETA ** (-jnp.arange(half, dtype=jnp.float32) / half)
    ang = pos[:, None] * inv[None, :]
    cos, sin = jnp.cos(ang), jnp.sin(ang)
    tf = t.astype(jnp.float32)
    t1, t2 = tf[..., :half], tf[..., half:]
    return jnp.concatenate([t1 * cos - t2 * sin, t2 * cos + t1 * sin], axis=-1).astype(t.dtype)


def moba_attention(q, k, v):
    B, H, S, Dh = q.shape
    nb = S // MOBA_BLOCK
    scale = Dh ** -0.5
    kb = k.reshape(B, H, nb, MOBA_BLOCK, Dh)
    vb = v.reshape(B, H, nb, MOBA_BLOCK, Dh)
    kmean = jnp.mean(kb.astype(jnp.float32), axis=3)
    gate = jnp.einsum('bhsd,bhnd->bhsn', q.astype(jnp.float32), kmean)
    qblk = jnp.arange(S) // MOBA_BLOCK
    past = jnp.arange(nb)[None, :] < qblk[:, None]
    gate = jnp.where(past[None, None], gate, NEG)
    k_sel = min(MOBA_TOPK, max(nb - 1, 1))
    _, idx = lax.top_k(gate, k_sel)
    valid = idx < qblk[None, None, :, None]
    bi = jnp.arange(B)[:, None, None, None]
    hi = jnp.arange(H)[None, :, None, None]
    QC = MOBA_QCHUNK

    def chunk(c):
        s0 = c * QC
        qc = lax.dynamic_slice_in_dim(q, s0, QC, axis=2)
        ic = lax.dynamic_slice_in_dim(idx, s0, QC, axis=2)
        vc = lax.dynamic_slice_in_dim(valid, s0, QC, axis=2)
        kg = kb[bi, hi, ic]
        vg = vb[bi, hi, ic]
        s_sel = jnp.einsum('bhqd,bhqnjd->bhqnj', qc, kg).astype(jnp.float32) * scale
        s_sel = jnp.where(vc[..., None], s_sel, NEG)
        blk = s0 // MOBA_BLOCK
        k_own = lax.dynamic_index_in_dim(kb, blk, axis=2, keepdims=False)
        v_own = lax.dynamic_index_in_dim(vb, blk, axis=2, keepdims=False)
        s_own = jnp.einsum('bhqd,bhjd->bhqj', qc, k_own).astype(jnp.float32) * scale
        qpos = s0 + jnp.arange(QC)
        kpos = blk * MOBA_BLOCK + jnp.arange(MOBA_BLOCK)
        s_own = jnp.where(kpos[None, :] <= qpos[:, None], s_own, NEG)
        logits = jnp.concatenate([s_sel.reshape(B, H, QC, k_sel * MOBA_BLOCK), s_own], axis=-1)
        p = jax.nn.softmax(logits, axis=-1).astype(v.dtype)
        p_sel = p[..., :k_sel * MOBA_BLOCK].reshape(B, H, QC, k_sel, MOBA_BLOCK)
        p_own = p[..., k_sel * MOBA_BLOCK:]
        return (jnp.einsum('bhqnj,bhqnjd->bhqd', p_sel, vg)
                + jnp.einsum('bhqj,bhjd->bhqd', p_own, v_own))

    outs = lax.map(chunk, jnp.arange(S // QC))
    return outs.transpose(1, 0, 3, 2, 4).reshape(B, S, H * Dh)


def ssd_scan(X, A, Bm, Cm):
    b, S, g, j, p = X.shape
    n = Bm.shape[-1]
    Q = SSD_CHUNK
    nc = S // Q
    X = X.reshape(b, nc, Q, g, j, p)
    A = A.reshape(b, nc, Q, g, j).transpose(0, 3, 4, 1, 2)
    Bc = Bm.reshape(b, nc, Q, g, n)
    Cc = Cm.reshape(b, nc, Q, g, n)
    A_cs = jnp.cumsum(A, axis=-1)
    causal = jnp.tril(jnp.ones((Q, Q), dtype=bool))
    Lmat = jnp.exp(jnp.where(causal, A_cs[..., :, None] - A_cs[..., None, :], -jnp.inf))
    CB = jnp.einsum('bclgn,bcsgn->bcgls', Cc, Bc)
    Y_diag = jnp.einsum('bcgls,bgjcls,bcsgjp->bclgjp', CB, Lmat, X)
    decay_states = jnp.exp(A_cs[..., -1:] - A_cs)
    states = jnp.einsum('bclgn,bgjcl,bclgjp->bcgjpn', Bc, decay_states, X)
    chunk_decay = jnp.exp(A_cs[..., -1])

    def step(h, inp):
        st, dec = inp
        return h * dec[..., None, None] + st, h

    h0 = jnp.zeros_like(states[:, 0])
    _, prev = lax.scan(step, h0, (states.transpose(1, 0, 2, 3, 4, 5), chunk_decay.transpose(3, 0, 1, 2)))
    prev = prev.transpose(1, 0, 2, 3, 4, 5)
    Y_off = jnp.einsum('bclgn,bcgjpn,bgjcl->bclgjp', Cc, prev, jnp.exp(A_cs))
    return (Y_diag + Y_off).reshape(b, S, g, j, p)


def ssd_mixer(z, xbc, dt_raw, conv_w, conv_b, dt_bias, a_log, d_skip, norm_w):
    B, S, _ = xbc.shape
    hpg = SSD_HEADS // SSD_GROUPS
    xbc = jax.nn.silu(causal_dwconv(xbc, conv_w, conv_b))
    xs, Bm, Cm = _split(xbc, (SSD_WIDTH, SSD_GROUPS * SSD_STATE, SSD_GROUPS * SSD_STATE))
    xs = xs.reshape(B, S, SSD_GROUPS, hpg, SSD_HEAD_DIM)
    Bm = Bm.reshape(B, S, SSD_GROUPS, SSD_STATE)
    Cm = Cm.reshape(B, S, SSD_GROUPS, SSD_STATE)
    dt = jax.nn.softplus(dt_raw.astype(jnp.float32) + dt_bias.astype(jnp.float32))
    A = -jnp.exp(a_log.astype(jnp.float32))
    dt_g = dt.reshape(B, S, SSD_GROUPS, hpg)
    y = ssd_scan(xs * dt_g[..., None], dt_g * A.reshape(SSD_GROUPS, hpg), Bm, Cm)
    y = y + xs * d_skip.reshape(SSD_GROUPS, hpg, 1)
    y = y.reshape(B, S, SSD_WIDTH) * jax.nn.silu(z.astype(jnp.float32))
    return group_rmsnorm(y, norm_w, SSD_GROUPS)


def memory_attention(q_m, mem_n, w_mem_kv):
    B, S, _ = q_m.shape
    q = q_m.reshape(B, S, MEM_HEADS, MEM_HEAD_DIM)
    k, v = jnp.split(mem_n @ w_mem_kv, 2, axis=-1)
    k = k.reshape(B, -1, MEM_HEADS, MEM_HEAD_DIM)
    v = v.reshape(B, -1, MEM_HEADS, MEM_HEAD_DIM)
    s = jnp.einsum('bshd,bmhd->bhsm', q, k).astype(jnp.float32) * (MEM_HEAD_DIM ** -0.5)
    p = jax.nn.softmax(s, axis=-1).astype(v.dtype)
    return jnp.einsum('bhsm,bmhd->bshd', p, v).reshape(B, S, MEM_WIDTH)


def token_mixing(h, mem_n, w_in, conv_ssd_w, conv_ssd_b, dt_bias, a_log, d_skip, ssd_norm,
                 w_mem_kv, w_br_attn, w_br_ssd, w_br_mem, w_out):
    B, S, _ = h.shape
    Sp = -(-S // MOBA_BLOCK) * MOBA_BLOCK
    hp = jnp.pad(h, ((0, 0), (0, Sp - S), (0, 0)))
    q_a, k_a, v_a, z, xbc, dt_raw, q_m, g_a, g_s, g_m = _split(hp @ w_in, IN_SIZES)
    heads = lambda t: t.reshape(B, Sp, ATT_HEADS, ATT_HEAD_DIM).transpose(0, 2, 1, 3)
    pos = jnp.arange(Sp, dtype=jnp.float32)
    y_a = moba_attention(rope(heads(q_a), pos), rope(heads(k_a), pos), heads(v_a)).astype(h.dtype)
    y_s = ssd_mixer(z, xbc, dt_raw, conv_ssd_w, conv_ssd_b, dt_bias, a_log, d_skip, ssd_norm).astype(h.dtype)
    y_m = memory_attention(q_m, mem_n, w_mem_kv)
    gate = lambda t: jax.nn.sigmoid(t.astype(jnp.float32)).astype(h.dtype)
    merged = (gate(g_a) * (y_a @ w_br_attn) + gate(g_s) * (y_s @ w_br_ssd)
              + gate(g_m) * (y_m @ w_br_mem))
    return (merged @ w_out)[:, :S]


def conv_glu_ffn(h, w_up, conv_w, conv_b, w_down):
    u = causal_dwconv(h @ w_up, conv_w, conv_b)
    a, g = jnp.split(u, 2, axis=-1)
    return (jax.nn.gelu(a, approximate=True) * g) @ w_down


def setup_inputs(seed: int = 0) -> dict:
    key = jax.random.key(seed)
    ks = jax.random.split(key, 24)
    f32 = jnp.float32
    L = DEPTH
    nrm = lambda k, shp, s: jax.random.normal(k, shp, f32) * s
    gain = lambda k: 1.0 + 0.05 * jax.random.normal(k, (L, D_MODEL), f32)
    dt0 = jnp.exp(jax.random.uniform(ks[10], (L, SSD_HEADS), f32, np.log(1e-3), np.log(1e-1)))
    return {
        'x': nrm(ks[0], (BATCH, SEQ, D_MODEL), 1.0),
        'mem': nrm(ks[1], (BATCH, MEM_LEN, D_MODEL), 1.0),
        'norm_mix_pre': gain(ks[2]),
        'norm_mix_post': gain(ks[3]),
        'norm_ffn_pre': gain(ks[4]),
        'norm_ffn_post': gain(ks[5]),
        'norm_mem': gain(ks[6]),
        'w_in': nrm(ks[7], (L, D_MODEL, N_IN), D_MODEL ** -0.5),
        'conv_ssd_w': nrm(ks[8], (L, SSD_CONV, SSD_CONV_DIM), SSD_CONV ** -0.5),
        'conv_ssd_b': nrm(ks[9], (L, SSD_CONV_DIM), 0.02),
        'dt_bias': dt0 + jnp.log(-jnp.expm1(-dt0)),
        'a_log': jnp.log(jax.random.uniform(ks[11], (L, SSD_HEADS), f32, 1.0, 16.0)),
        'd_skip': 1.0 + 0.1 * jax.random.normal(ks[12], (L, SSD_HEADS), f32),
        'ssd_norm': 1.0 + 0.05 * jax.random.normal(ks[13], (L, SSD_WIDTH), f32),
        'w_mem_kv': nrm(ks[14], (L, D_MODEL, 2 * MEM_WIDTH), D_MODEL ** -0.5),
        'w_br_attn': nrm(ks[15], (L, ATT_WIDTH, D_MODEL), ATT_WIDTH ** -0.5),
        'w_br_ssd': nrm(ks[16], (L, SSD_WIDTH, D_MODEL), SSD_WIDTH ** -0.5),
        'w_br_mem': nrm(ks[17], (L, MEM_WIDTH, D_MODEL), MEM_WIDTH ** -0.5),
        'w_out': nrm(ks[18], (L, D_MODEL, D_MODEL), D_MODEL ** -0.5),
        'w_up': nrm(ks[19], (L, D_MODEL, 2 * D_FF), D_MODEL ** -0.5),
        'conv_ffn_w': nrm(ks[20], (L, FFN_CONV, 2 * D_FF), FFN_CONV ** -0.5),
        'conv_ffn_b': nrm(ks[21], (L, 2 * D_FF), 0.02),
        'w_down': nrm(ks[22], (L, D_FF, D_MODEL), D_FF ** -0.5),
    }


def reference(x, mem, norm_mix_pre, norm_mix_post, norm_ffn_pre, norm_ffn_post, norm_mem,
              w_in, conv_ssd_w, conv_ssd_b, dt_bias, a_log, d_skip, ssd_norm, w_mem_kv,
              w_br_attn, w_br_ssd, w_br_mem, w_out, w_up, conv_ffn_w, conv_ffn_b, w_down):
    for l in range(DEPTH):
        h = rmsnorm(x, norm_mix_pre[l])
        mem_n = rmsnorm(mem, norm_mem[l])
        y = token_mixing(h, mem_n, w_in[l], conv_ssd_w[l], conv_ssd_b[l], dt_bias[l], a_log[l],
                         d_skip[l], ssd_norm[l], w_mem_kv[l], w_br_attn[l], w_br_ssd[l],
                         w_br_mem[l], w_out[l])
        x = x + rmsnorm(y, norm_mix_post[l]).astype(x.dtype)
        h = rmsnorm(x, norm_ffn_pre[l])
        y = conv_glu_ffn(h, w_up[l], conv_ffn_w[l], conv_ffn_b[l], w_down[l])
        x = x + rmsnorm(y, norm_ffn_post[l]).astype(x.dtype)
    return x
```

```python
import functools

import jax
import jax.numpy as jnp
from jax import lax
from jax.experimental import pallas as pl
from jax.experimental.pallas import tpu as pltpu

F32 = jnp.float32
BF16 = jnp.bfloat16

EPS = 1e-6
NEG = -1e30
ROPE_THETA = 10000.0

ATT_HEADS = 8
ATT_HEAD_DIM = 128
ATT_WIDTH = ATT_HEADS * ATT_HEAD_DIM
MOBA_BLOCK = 256
MOBA_TOPK = 3

SSD_HEAD_DIM = 64
SSD_GROUPS = 4
SSD_STATE = 128
SSD_CONV = 4
SSD_CHUNK = 256

MEM_HEADS = 4
MEM_HEAD_DIM = 256
FFN_CONV = 3

LANES = 128
HALO = 16
VMEM_LIMIT = 56 * 1024 * 1024


def _cparams(semantics, vmem=VMEM_LIMIT):
    return pltpu.CompilerParams(dimension_semantics=semantics, vmem_limit_bytes=vmem)


def _rms(x, gain):
    return x * lax.rsqrt(jnp.mean(x * x, axis=-1, keepdims=True) + EPS) * gain


def _norm_matmul_kernel(x_ref, g_ref, w_ref, o_ref, hn_ref):
    @pl.when(pl.program_id(1) == 0)
    def _():
        hn_ref[...] = _rms(x_ref[...], g_ref[...]).astype(BF16)

    o_ref[...] = jnp.dot(hn_ref[...], w_ref[...], preferred_element_type=F32).astype(o_ref.dtype)


def norm_matmul(x, gain, w, out_dtype, tm, tn):
    m, d = x.shape
    n = w.shape[1]
    return pl.pallas_call(
        _norm_matmul_kernel,
        out_shape=jax.ShapeDtypeStruct((m, n), out_dtype),
        grid=(m // tm, n // tn),
        in_specs=[pl.BlockSpec((tm, d), lambda i, j: (i, 0)),
                  pl.BlockSpec((1, d), lambda i, j: (0, 0)),
                  pl.BlockSpec((d, tn), lambda i, j: (0, j))],
        out_specs=pl.BlockSpec((tm, tn), lambda i, j: (i, j)),
        scratch_shapes=[pltpu.VMEM((tm, d), BF16)],
        compiler_params=_cparams(("parallel", "arbitrary")),
        name="norm_matmul",
    )(x, gain.reshape(1, d), w)


def _rope_kernel(q_ref, k_ref, v_ref, cos_ref, sin_ref, qo_ref, ko_ref, vo_ref, km_ref):
    cos = cos_ref[...]
    sin = sin_ref[...]
    half = ATT_HEAD_DIM // 2
    for h in range(ATT_HEADS):
        sl = slice(h * ATT_HEAD_DIM, (h + 1) * ATT_HEAD_DIM)
        q = q_ref[:, sl]
        k = k_ref[:, sl]
        qr = q * cos + pltpu.roll(q, half, 1) * sin
        kr = k * cos + pltpu.roll(k, half, 1) * sin
        qo_ref[:, sl] = qr.astype(BF16)
        ko_ref[:, sl] = kr.astype(BF16)
        km_ref[0, :, sl] = jnp.mean(kr, axis=0, keepdims=True)
    vo_ref[...] = v_ref[...].astype(BF16)


def rope_kmean(proj, cos, sin, q_col, seq):
    m = proj.shape[0]
    nblk = m // MOBA_BLOCK
    per_seq = seq // MOBA_BLOCK
    row = lambda c: pl.BlockSpec((MOBA_BLOCK, ATT_WIDTH), lambda i: (i, c))
    tab = pl.BlockSpec((MOBA_BLOCK, ATT_HEAD_DIM), lambda i: (i % per_seq, 0))
    out = pl.BlockSpec((MOBA_BLOCK, ATT_WIDTH), lambda i: (i, 0))
    return pl.pallas_call(
        _rope_kernel,
        out_shape=(jax.ShapeDtypeStruct((m, ATT_WIDTH), BF16),) * 3
        + (jax.ShapeDtypeStruct((nblk, 1, ATT_WIDTH), F32),),
        grid=(nblk,),
        in_specs=[row(q_col), row(q_col + 1), row(q_col + 2), tab, tab],
        out_specs=(out, out, out, pl.BlockSpec((1, 1, ATT_WIDTH), lambda i: (i, 0, 0))),
        compiler_params=_cparams(("parallel",)),
        name="rope_kmean",
    )(proj, proj, proj, cos, sin)


def _moba_kernel(q_ref, k_ref, v_ref, km_ref, o_ref, selb_ref, *, nblk):
    i = pl.program_id(2)
    blk = MOBA_BLOCK
    scale = ATT_HEAD_DIM ** -0.5
    q = q_ref[...]
    nt = (((1,), (1,)), ((), ()))

    km = km_ref[...]
    km_hi = km.astype(BF16)
    r1 = km - km_hi.astype(F32)
    km_mid = r1.astype(BF16)
    km_lo = (r1 - km_mid.astype(F32)).astype(BF16)
    gate = (lax.dot_general(q, km_hi, nt, preferred_element_type=F32)
            + lax.dot_general(q, km_mid, nt, preferred_element_type=F32)
            + lax.dot_general(q, km_lo, nt, preferred_element_type=F32))
    col = lax.broadcasted_iota(jnp.int32, (blk, nblk), 1).astype(F32)
    past = col < i.astype(F32)
    g = jnp.where(past, gate, NEG)
    bias = jnp.full((blk, nblk), NEG, F32)
    for _ in range(min(MOBA_TOPK, max(nblk - 1, 1))):
        mx = jnp.max(g, axis=1, keepdims=True)
        first = jnp.min(jnp.where(g == mx, col, float(nblk)), axis=1, keepdims=True)
        pick = col == first
        bias = jnp.where(pick, 0.0, bias)
        g = jnp.where(pick, -jnp.inf, g)
    selb_ref[...] = jnp.where(past, bias, NEG).astype(BF16)

    k0 = k_ref[pl.ds(pl.multiple_of(i * blk, blk), blk), :]
    v0 = v_ref[pl.ds(pl.multiple_of(i * blk, blk), blk), :]
    s = lax.dot_general(q, k0, nt, preferred_element_type=F32) * scale
    rows = lax.broadcasted_iota(jnp.int32, (blk, blk), 0)
    cols = lax.broadcasted_iota(jnp.int32, (blk, blk), 1)
    s = jnp.where(cols <= rows, s, NEG)
    m0 = jnp.max(s, axis=1, keepdims=True)
    p = jnp.exp(s - m0)
    l0 = jnp.sum(p, axis=1, keepdims=True)
    acc0 = jnp.dot(p.astype(BF16), v0, preferred_element_type=F32)

    blk_row = lax.broadcasted_iota(jnp.int32, (nblk, blk), 0)

    def body(j, carry):
        m, l, acc = carry
        start = pl.multiple_of(j * blk, blk)
        kj = k_ref[pl.ds(start, blk), :]
        vj = v_ref[pl.ds(start, blk), :]
        onehot = jnp.where(blk_row == j, 1.0, 0.0).astype(BF16)
        s = (lax.dot_general(q, kj, nt, preferred_element_type=F32) * scale
             + jnp.dot(selb_ref[...], onehot, preferred_element_type=F32))
        m_new = jnp.maximum(m, jnp.max(s, axis=1, keepdims=True))
        alpha = jnp.exp(m - m_new)
        p = jnp.exp(s - m_new)
        l = alpha * l + jnp.sum(p, axis=1, keepdims=True)
        acc = alpha * acc + jnp.dot(p.astype(BF16), vj, preferred_element_type=F32)
        return m_new, l, acc

    m, l, acc = lax.fori_loop(0, i, body, (m0, l0, acc0))
    o_ref[...] = (acc / l).astype(o_ref.dtype)


def moba_attention(qr, kr, vb, kmean, batch, seq):
    nblk = seq // MOBA_BLOCK
    qspec = pl.BlockSpec((MOBA_BLOCK, ATT_HEAD_DIM), lambda b, h, i: (b * nblk + i, h))
    kvspec = pl.BlockSpec((seq, ATT_HEAD_DIM), lambda b, h, i: (b, h))
    return pl.pallas_call(
        functools.partial(_moba_kernel, nblk=nblk),
        out_shape=jax.ShapeDtypeStruct(qr.shape, BF16),
        grid=(batch, ATT_HEADS, nblk),
        in_specs=[qspec, kvspec, kvspec,
                  pl.BlockSpec((None, nblk, ATT_HEAD_DIM), lambda b, h, i: (b, 0, h))],
        out_specs=qspec,
        scratch_shapes=[pltpu.VMEM((MOBA_BLOCK, nblk), BF16)],
        compiler_params=_cparams(("parallel", "parallel", "arbitrary")),
        name="moba_attn",
    )(qr, kr, vb, kmean)


def _silu(x):
    return x * (1.0 / (1.0 + jnp.exp(-x)))


def _ssd_kernel(z_ref, xs_ref, bc_ref, dt_ref, cwx_ref, cbx_ref, cwb_ref, cbb_ref, dtb_ref,
                alog_ref, dsk_ref, nw_ref, o_ref, xext_ref, bext_ref, state_ref, y_ref, *, heads):
    c = pl.program_id(1)
    q = SSD_CHUNK
    p = SSD_HEAD_DIM
    n = SSD_STATE
    hpg = heads // SSD_GROUPS
    gw = hpg * p

    @pl.when(c == 0)
    def _():
        xext_ref[0:HALO, :] = jnp.zeros((HALO, xext_ref.shape[1]), F32)
        bext_ref[0:HALO, :] = jnp.zeros((HALO, bext_ref.shape[1]), F32)
        state_ref[...] = jnp.zeros(state_ref.shape, F32)

    def conv_silu(ext_ref, raw, w_ref, b_ref):
        ext_ref[HALO:HALO + q, :] = raw
        acc = b_ref[...]
        for kk in range(SSD_CONV):
            off = HALO - (SSD_CONV - 1) + kk
            acc = acc + ext_ref[pl.ds(off, q), :] * w_ref[kk:kk + 1, :]
        ext_ref[0:HALO, :] = raw[q - HALO:, :]
        return _silu(acc)

    xs = conv_silu(xext_ref, xs_ref[...], cwx_ref, cbx_ref)
    bc = conv_silu(bext_ref, bc_ref[...], cwb_ref, cbb_ref)

    dt = dt_ref[...] + dtb_ref[...]
    dt = jnp.maximum(dt, 0.0) + jnp.log1p(jnp.exp(-jnp.abs(dt)))
    a = -jnp.exp(alog_ref[...])
    da = dt * a
    tri = (lax.broadcasted_iota(jnp.int32, (q, q), 0)
           >= lax.broadcasted_iota(jnp.int32, (q, q), 1))
    acs = jnp.dot(tri.astype(F32), da, preferred_element_type=F32,
                  precision=lax.Precision.HIGHEST)
    acs_t = acs.T
    a_last = acs[q - 1:q, :]
    dec_chunk = jnp.exp(a_last)
    dec_in = jnp.exp(acs)
    dec_out = jnp.exp(a_last - acs)

    for g in range(SSD_GROUPS):
        b_g = bc[:, g * n:(g + 1) * n]
        c_g = bc[:, (SSD_GROUPS + g) * n:(SSD_GROUPS + g + 1) * n]
        b_t = b_g.T.astype(BF16)
        c_b = c_g.astype(BF16)
        cb = jnp.dot(c_b, b_t, preferred_element_type=F32)
        st_g = state_ref[g]
        y_off = jnp.dot(c_b, st_g.astype(BF16), preferred_element_type=F32)
        xd_parts = []
        for j in range(hpg):
            h = g * hpg + j
            x_h = xs[:, h * p:(h + 1) * p]
            xdt = x_h * dt[:, h:h + 1]
            lmat = jnp.where(tri, jnp.exp(acs[:, h:h + 1] - acs_t[h:h + 1, :]), 0.0)
            y_d = jnp.dot((cb * lmat).astype(BF16), xdt.astype(BF16), preferred_element_type=F32)
            y_h = y_d + y_off[:, j * p:(j + 1) * p] * dec_in[:, h:h + 1] + x_h * dsk_ref[:, h:h + 1]
            y_ref[:, h * p:(h + 1) * p] = y_h
            xd_parts.append((xdt * dec_out[:, h:h + 1]).astype(BF16))
        xd = jnp.concatenate(xd_parts, axis=1)
        st_new = jnp.dot(b_t, xd, preferred_element_type=F32)
        dec_parts = [jnp.broadcast_to(dec_chunk[:, g * hpg + j:g * hpg + j + 1], (n, p))
                     for j in range(hpg)]
        state_ref[g] = st_g * jnp.concatenate(dec_parts, axis=1) + st_new

    zz = z_ref[...]
    y = y_ref[...] * _silu(zz)
    for g in range(SSD_GROUPS):
        yg = y[:, g * gw:(g + 1) * gw]
        yg = yg * lax.rsqrt(jnp.mean(yg * yg, axis=-1, keepdims=True) + EPS)
        o_ref[:, g * gw:(g + 1) * gw] = (yg * nw_ref[:, g * gw:(g + 1) * gw]).astype(o_ref.dtype)


def ssd_mixer(proj, dtp, z_col, xs_col, bc_col, dt_col, conv_w, conv_b, dt_bias, a_log, d_skip,
              norm_w, batch, seq, width):
    m = proj.shape[0]
    heads = width // SSD_HEAD_DIM
    nc = seq // SSD_CHUNK
    bcw = 2 * SSD_GROUPS * SSD_STATE
    pad = lambda v: jnp.pad(v.astype(F32), (0, LANES - heads)).reshape(1, LANES)
    row = lambda w, col: pl.BlockSpec((SSD_CHUNK, w), lambda b, c: (b * nc + c, col))
    full = lambda r, w: pl.BlockSpec((r, w), lambda b, c: (0, 0))
    return pl.pallas_call(
        functools.partial(_ssd_kernel, heads=heads),
        out_shape=jax.ShapeDtypeStruct((m, width), BF16),
        grid=(batch, nc),
        in_specs=[row(width, z_col), row(width, xs_col), row(bcw, bc_col), row(LANES, dt_col),
                  full(SSD_CONV, width), full(1, width), full(SSD_CONV, bcw), full(1, bcw),
                  full(1, LANES), full(1, LANES), full(1, LANES), full(1, width)],
        out_specs=pl.BlockSpec((SSD_CHUNK, width), lambda b, c: (b * nc + c, 0)),
        scratch_shapes=[pltpu.VMEM((HALO + SSD_CHUNK, width), F32),
                        pltpu.VMEM((HALO + SSD_CHUNK, bcw), F32),
                        pltpu.VMEM((SSD_GROUPS, SSD_STATE, width // SSD_GROUPS), F32),
                        pltpu.VMEM((SSD_CHUNK, width), F32)],
        compiler_params=_cparams(("parallel", "arbitrary")),
        name="ssd",
    )(proj, proj, proj, dtp, conv_w[:, :width], conv_b[:width].reshape(1, width),
      conv_w[:, width:], conv_b[width:].reshape(1, bcw), pad(dt_bias), pad(a_log), pad(d_skip),
      norm_w.reshape(1, width))


def _mem_attn_kernel(q_ref, k_ref, v_ref, o_ref):
    scale = MEM_HEAD_DIM ** -0.5
    nt = (((1,), (1,)), ((), ()))
    for h in range(MEM_HEADS):
        sl = slice(h * MEM_HEAD_DIM, (h + 1) * MEM_HEAD_DIM)
        q = q_ref[:, sl].astype(BF16)
        s = lax.dot_general(q, k_ref[:, sl], nt, preferred_element_type=F32) * scale
        p = jnp.exp(s - jnp.max(s, axis=1, keepdims=True))
        l = jnp.sum(p, axis=1, keepdims=True)
        o = jnp.dot(p.astype(BF16), v_ref[:, sl], preferred_element_type=F32)
        o_ref[:, sl] = (o / l).astype(o_ref.dtype)


def mem_attention(qp, kv, batch, seq, tq):
    w = MEM_HEADS * MEM_HEAD_DIM
    mem_len = kv.shape[0] // batch
    nq = seq // tq
    return pl.pallas_call(
        _mem_attn_kernel,
        out_shape=jax.ShapeDtypeStruct((batch * seq, w), BF16),
        grid=(batch, nq),
        in_specs=[pl.BlockSpec((tq, w), lambda b, i: (b * nq + i, 0)),
                  pl.BlockSpec((mem_len, w), lambda b, i: (b, 0)),
                  pl.BlockSpec((mem_len, w), lambda b, i: (b, 1))],
        out_specs=pl.BlockSpec((tq, w), lambda b, i: (b * nq + i, 0)),
        compiler_params=_cparams(("parallel", "parallel")),
        name="mem_attn",
    )(qp, kv, kv)


def _sigmoid(x):
    return 1.0 / (1.0 + jnp.exp(-x))


def _merge_kernel(ya_ref, ys_ref, ym_ref, ga_ref, gs_ref, gm_ref, wa_ref, ws_ref, wm_ref, o_ref):
    acc = _sigmoid(ga_ref[...]) * jnp.dot(ya_ref[...], wa_ref[...], preferred_element_type=F32)
    acc += _sigmoid(gs_ref[...]) * jnp.dot(ys_ref[...], ws_ref[...], preferred_element_type=F32)
    acc += _sigmoid(gm_ref[...]) * jnp.dot(ym_ref[...], wm_ref[...], preferred_element_type=F32)
    o_ref[...] = acc.astype(o_ref.dtype)


def branch_merge(ya, ys, ym, proj, gate_col, wa, ws, wm, tm, tn):
    m = ya.shape[0]
    d = wa.shape[1]
    per = d // tn
    g0 = gate_col // tn
    yspec = lambda w: pl.BlockSpec((tm, w), lambda i, j: (i, 0))
    gspec = lambda k: pl.BlockSpec((tm, tn), lambda i, j: (i, g0 + k * per + j))
    wspec = lambda w: pl.BlockSpec((w, tn), lambda i, j: (0, j))
    return pl.pallas_call(
        _merge_kernel,
        out_shape=jax.ShapeDtypeStruct((m, d), BF16),
        grid=(m // tm, per),
        in_specs=[yspec(ya.shape[1]), yspec(ys.shape[1]), yspec(ym.shape[1]),
                  gspec(0), gspec(1), gspec(2),
                  wspec(wa.shape[0]), wspec(ws.shape[0]), wspec(wm.shape[0])],
        out_specs=pl.BlockSpec((tm, tn), lambda i, j: (i, j)),
        compiler_params=_cparams(("parallel", "arbitrary")),
        name="branch_merge",
    )(ya, ys, ym, proj, proj, proj, wa, ws, wm)


def _out_residual_kernel(a_ref, w_ref, x_ref, g_ref, o_ref):
    y = jnp.dot(a_ref[...], w_ref[...], preferred_element_type=F32)
    o_ref[...] = x_ref[...] + _rms(y, g_ref[...])


def out_residual(a, w, x, gain, tm):
    m, d = x.shape
    return pl.pallas_call(
        _out_residual_kernel,
        out_shape=jax.ShapeDtypeStruct((m, d), F32),
        grid=(m // tm,),
        in_specs=[pl.BlockSpec((tm, a.shape[1]), lambda i: (i, 0)),
                  pl.BlockSpec(w.shape, lambda i: (0, 0)),
                  pl.BlockSpec((tm, d), lambda i: (i, 0)),
                  pl.BlockSpec((1, d), lambda i: (0, 0))],
        out_specs=pl.BlockSpec((tm, d), lambda i: (i, 0)),
        input_output_aliases={2: 0},
        compiler_params=_cparams(("parallel",)),
        name="out_residual",
    )(a, w, x, gain.reshape(1, d))


def _gelu_tanh(x):
    return 0.5 * x * (1.0 + jnp.tanh(0.7978845608028654 * (x + 0.044715 * x * x * x)))


def _ffn_kernel(x_ref, xh_ref, gpre_ref, wa_ref, wg_ref, cwa_ref, cwg_ref, cba_ref, cbg_ref,
                wd_ref, gpost_ref, o_ref, hn_ref, u_ref, acc_ref, *, tm, seq):
    i = pl.program_id(0)
    j = pl.program_id(1)

    @pl.when(j == 0)
    def _():
        hn_ref[0:HALO, :] = _rms(xh_ref[...], gpre_ref[...]).astype(BF16)
        hn_ref[HALO:HALO + tm, :] = _rms(x_ref[...], gpre_ref[...]).astype(BF16)
        acc_ref[...] = jnp.zeros(acc_ref.shape, F32)

    seq_start = (i * tm) % seq == 0

    def conv(w_ref, cw_ref, cb_ref):
        u = jnp.dot(hn_ref[...], w_ref[...], preferred_element_type=F32)
        u_ref[0:HALO, :] = jnp.where(seq_start, 0.0, u[0:HALO, :])
        u_ref[HALO:HALO + tm, :] = u[HALO:, :]
        out = cb_ref[...]
        for kk in range(FFN_CONV):
            off = HALO - (FFN_CONV - 1) + kk
            out = out + u_ref[pl.ds(off, tm), :] * cw_ref[kk:kk + 1, :]
        return out

    a = conv(wa_ref, cwa_ref, cba_ref)
    g = conv(wg_ref, cwg_ref, cbg_ref)
    act = (_gelu_tanh(a) * g).astype(BF16)
    acc_ref[...] += jnp.dot(act, wd_ref[...], preferred_element_type=F32)

    @pl.when(j == pl.num_programs(1) - 1)
    def _():
        o_ref[...] = x_ref[...] + _rms(acc_ref[...], gpost_ref[...])


def conv_glu_ffn(x, gpre, w_up, conv_w, conv_b, w_down, gpost, seq, tm, tf):
    m, d = x.shape
    dff = w_down.shape[0]
    nf = dff // tf
    hb = tm // HALO
    vec = lambda: pl.BlockSpec((1, d), lambda i, j: (0, 0))
    return pl.pallas_call(
        functools.partial(_ffn_kernel, tm=tm, seq=seq),
        out_shape=jax.ShapeDtypeStruct((m, d), F32),
        grid=(m // tm, nf),
        in_specs=[pl.BlockSpec((tm, d), lambda i, j: (i, 0)),
                  pl.BlockSpec((HALO, d), lambda i, j: (jnp.maximum(i * hb - 1, 0), 0)),
                  vec(),
                  pl.BlockSpec((d, tf), lambda i, j: (0, j)),
                  pl.BlockSpec((d, tf), lambda i, j: (0, nf + j)),
                  pl.BlockSpec((FFN_CONV, tf), lambda i, j: (0, j)),
                  pl.BlockSpec((FFN_CONV, tf), lambda i, j: (0, nf + j)),
                  pl.BlockSpec((1, tf), lambda i, j: (0, j)),
                  pl.BlockSpec((1, tf), lambda i, j: (0, nf + j)),
                  pl.BlockSpec((tf, d), lambda i, j: (j, 0)),
                  vec()],
        out_specs=pl.BlockSpec((tm, d), lambda i, j: (i, 0)),
        scratch_shapes=[pltpu.VMEM((HALO + tm, d), BF16),
                        pltpu.VMEM((HALO + tm, tf), F32),
                        pltpu.VMEM((tm, d), F32)],
        compiler_params=_cparams(("parallel", "arbitrary")),
        name="conv_glu_ffn",
    )(x, x, gpre.reshape(1, d), w_up, w_up, conv_w, conv_w, conv_b.reshape(1, -1),
      conv_b.reshape(1, -1), w_down, gpost.reshape(1, d))


def _rope_tables(seq):
    half = ATT_HEAD_DIM // 2
    inv = ROPE_THETA ** (-jnp.arange(half, dtype=F32) / half)
    ang = jnp.arange(seq, dtype=F32)[:, None] * inv[None, :]
    cos, sin = jnp.cos(ang), jnp.sin(ang)
    return jnp.concatenate([cos, cos], axis=1), jnp.concatenate([-sin, sin], axis=1)


def kernel(x, mem, norm_mix_pre, norm_mix_post, norm_ffn_pre, norm_ffn_post, norm_mem, w_in,
           conv_ssd_w, conv_ssd_b, dt_bias, a_log, d_skip, ssd_norm, w_mem_kv, w_br_attn, w_br_ssd,
           w_br_mem, w_out, w_up, conv_ffn_w, conv_ffn_b, w_down):
    batch, seq, d = x.shape
    depth = w_in.shape[0]
    mem_len = mem.shape[1]
    ssd_w = w_br_ssd.shape[1]
    heads = ssd_w // SSD_HEAD_DIM
    bcw = 2 * SSD_GROUPS * SSD_STATE
    mem_w = MEM_HEADS * MEM_HEAD_DIM
    assert seq % MOBA_BLOCK == 0 and seq % SSD_CHUNK == 0 and d == ssd_w

    o_q, o_z = 0, 3 * ATT_WIDTH
    o_xs = o_z + ssd_w
    o_bc = o_xs + ssd_w
    o_dt = o_bc + bcw
    o_qm = o_dt + heads
    o_g = o_qm + mem_w

    cos, sin = _rope_tables(seq)
    xf = x.reshape(batch * seq, d)
    memf = mem.reshape(batch * mem_len, d)

    for l in range(depth):
        wl = w_in[l]
        w_a = jnp.concatenate([wl[:, o_z:o_xs], wl[:, o_g:], wl[:, o_xs:o_dt], wl[:, o_q:o_z]],
                              axis=1).astype(BF16)
        w_b = jnp.concatenate([wl[:, o_qm:o_g], wl[:, o_dt:o_qm],
                               jnp.zeros((d, LANES - heads), F32)], axis=1).astype(BF16)
        proj = norm_matmul(xf, norm_mix_pre[l], w_a, F32, tm=1024, tn=512)
        projb = norm_matmul(xf, norm_mix_pre[l], w_b, F32, tm=1024, tn=w_b.shape[1])
        c_z, c_gate = 0, ssd_w
        c_xs = (4 * d) // ssd_w
        c_bc = (4 * d + ssd_w) // bcw
        c_q = (4 * d + ssd_w + bcw) // ATT_WIDTH

        qr, kr, vb, kmean = rope_kmean(proj, cos, sin, c_q, seq)
        y_a = moba_attention(qr, kr, vb, kmean.reshape(batch, seq // MOBA_BLOCK, ATT_WIDTH),
                             batch, seq)
        y_s = ssd_mixer(proj, projb, c_z, c_xs, c_bc, mem_w // LANES, conv_ssd_w[l], conv_ssd_b[l],
                        dt_bias[l], a_log[l], d_skip[l], ssd_norm[l], batch, seq, ssd_w)
        kv = norm_matmul(memf, norm_mem[l], w_mem_kv[l].astype(BF16), BF16,
                         tm=batch * mem_len, tn=512)
        y_m = mem_attention(projb, kv, batch, seq, tq=512)
        merged = branch_merge(y_a, y_s, y_m, proj, c_gate, w_br_attn[l].astype(BF16),
                              w_br_ssd[l].astype(BF16), w_br_mem[l].astype(BF16), tm=1024, tn=512)
        xf = out_residual(merged, w_out[l].astype(BF16), xf, norm_mix_post[l], tm=512)
        xf = conv_glu_ffn(xf, norm_ffn_pre[l], w_up[l].astype(BF16), conv_ffn_w[l], conv_ffn_b[l],
                          w_down[l].astype(BF16), norm_ffn_post[l], seq, tm=512, tf=512)
    return xf.reshape(batch, seq, d)
```

```python
import functools

import jax
import jax.numpy as jnp
from jax import lax
from jax.experimental import pallas as pl
from jax.experimental.pallas import tpu as pltpu

F32 = jnp.float32
BF16 = jnp.bfloat16

EPS = 1e-6
NEG = -1e30
ROPE_THETA = 10000.0

ATT_HEADS = 8
ATT_HEAD_DIM = 128
ATT_WIDTH = ATT_HEADS * ATT_HEAD_DIM
MOBA_BLOCK = 256
MOBA_TOPK = 3
Q_PRESCALE = ATT_HEAD_DIM ** -0.5 * 1.4426950408889634

SSD_HEAD_DIM = 64
SSD_GROUPS = 4
SSD_STATE = 128
SSD_CONV = 4
SSD_CHUNK = 256

MEM_HEADS = 4
MEM_HEAD_DIM = 256
FFN_CONV = 3

LANES = 128
HALO = 16
VMEM_LIMIT = 56 * 1024 * 1024


def _cparams(semantics, vmem=VMEM_LIMIT):
    return pltpu.CompilerParams(dimension_semantics=semantics, vmem_limit_bytes=vmem)


def _rms(x, gain):
    return x * lax.rsqrt(jnp.mean(x * x, axis=-1, keepdims=True) + EPS) * gain


def _norm_matmul_kernel(x_ref, g_ref, w_ref, o_ref, hn_ref):
    @pl.when(pl.program_id(1) == 0)
    def _():
        hn_ref[...] = _rms(x_ref[...], g_ref[...]).astype(BF16)

    o_ref[...] = jnp.dot(hn_ref[...], w_ref[...], preferred_element_type=F32).astype(o_ref.dtype)


def norm_matmul(x, gain, w, out_dtype, tm, tn):
    m, d = x.shape
    n = w.shape[1]
    return pl.pallas_call(
        _norm_matmul_kernel,
        out_shape=jax.ShapeDtypeStruct((m, n), out_dtype),
        grid=(m // tm, n // tn),
        in_specs=[pl.BlockSpec((tm, d), lambda i, j: (i, 0)),
                  pl.BlockSpec((1, d), lambda i, j: (0, 0)),
                  pl.BlockSpec((d, tn), lambda i, j: (0, j))],
        out_specs=pl.BlockSpec((tm, tn), lambda i, j: (i, j)),
        scratch_shapes=[pltpu.VMEM((tm, d), BF16)],
        compiler_params=_cparams(("parallel", "arbitrary")),
        name="norm_matmul",
    )(x, gain.reshape(1, d), w)


def _rope_kernel(q_ref, k_ref, v_ref, cos_ref, sin_ref, qo_ref, ko_ref, vo_ref, km_ref):
    cos = cos_ref[...]
    sin = sin_ref[...]
    half = ATT_HEAD_DIM // 2
    for h in range(ATT_HEADS):
        sl = slice(h * ATT_HEAD_DIM, (h + 1) * ATT_HEAD_DIM)
        q = q_ref[:, sl]
        k = k_ref[:, sl]
        qr = q * cos + pltpu.roll(q, half, 1) * sin
        kr = k * cos + pltpu.roll(k, half, 1) * sin
        qo_ref[:, sl] = (qr * Q_PRESCALE).astype(BF16)
        ko_ref[:, sl] = kr.astype(BF16)
        km_ref[0, :, sl] = jnp.mean(kr, axis=0, keepdims=True)
    vo_ref[...] = v_ref[...].astype(BF16)


def rope_kmean(proj, cos, sin, q_col, seq):
    m = proj.shape[0]
    nblk = m // MOBA_BLOCK
    per_seq = seq // MOBA_BLOCK
    row = lambda c: pl.BlockSpec((MOBA_BLOCK, ATT_WIDTH), lambda i: (i, c))
    tab = pl.BlockSpec((MOBA_BLOCK, ATT_HEAD_DIM), lambda i: (i % per_seq, 0))
    out = pl.BlockSpec((MOBA_BLOCK, ATT_WIDTH), lambda i: (i, 0))
    return pl.pallas_call(
        _rope_kernel,
        out_shape=(jax.ShapeDtypeStruct((m, ATT_WIDTH), BF16),) * 3
        + (jax.ShapeDtypeStruct((nblk, 1, ATT_WIDTH), F32),),
        grid=(nblk,),
        in_specs=[row(q_col), row(q_col + 1), row(q_col + 2), tab, tab],
        out_specs=(out, out, out, pl.BlockSpec((1, 1, ATT_WIDTH), lambda i: (i, 0, 0))),
        compiler_params=_cparams(("parallel",)),
        name="rope_kmean",
    )(proj, proj, proj, cos, sin)


def _moba_head(i, q, k_ref, v_ref, km, selb_ref, lanes, nblk):
    blk = MOBA_BLOCK
    nt = (((1,), (1,)), ((), ()))

    km_hi = km.astype(BF16)
    r1 = km - km_hi.astype(F32)
    km_mid = r1.astype(BF16)
    km_lo = (r1 - km_mid.astype(F32)).astype(BF16)
    gate = (lax.dot_general(q, km_hi, nt, preferred_element_type=F32)
            + lax.dot_general(q, km_mid, nt, preferred_element_type=F32)
            + lax.dot_general(q, km_lo, nt, preferred_element_type=F32))
    col = lax.broadcasted_iota(jnp.int32, (blk, nblk), 1).astype(F32)
    past = col < i.astype(F32)
    g = jnp.where(past, gate, NEG)
    bias = jnp.full((blk, nblk), NEG, F32)
    for _ in range(min(MOBA_TOPK, max(nblk - 1, 1))):
        mx = jnp.max(g, axis=1, keepdims=True)
        first = jnp.min(jnp.where(g == mx, col, float(nblk)), axis=1, keepdims=True)
        pick = col == first
        bias = jnp.where(pick, 0.0, bias)
        g = jnp.where(pick, -jnp.inf, g)
    selb_ref[...] = jnp.where(past, bias, NEG).astype(BF16)

    start = pl.multiple_of(i * blk, blk)
    s = lax.dot_general(q, k_ref[pl.ds(start, blk), lanes], nt, preferred_element_type=F32)
    rows = lax.broadcasted_iota(jnp.int32, (blk, blk), 0)
    cols = lax.broadcasted_iota(jnp.int32, (blk, blk), 1)
    s = jnp.where(cols <= rows, s, NEG)
    m0 = jnp.max(s, axis=1, keepdims=True)
    p = jnp.exp2(s - m0)
    l0 = jnp.sum(p, axis=1, keepdims=True)
    acc0 = jnp.dot(p.astype(BF16), v_ref[pl.ds(start, blk), lanes], preferred_element_type=F32)
    return m0, l0, acc0


def _moba_kernel(q_ref, k_ref, v_ref, km_ref, o_ref, selb_ref, *, nblk, hps, kt):
    i = pl.program_id(2)
    blk = MOBA_BLOCK
    tk = kt * blk
    nt = (((1,), (1,)), ((), ()))
    lanes = [slice(a * ATT_HEAD_DIM, (a + 1) * ATT_HEAD_DIM) for a in range(hps)]
    qs = [q_ref[:, lanes[a]] for a in range(hps)]
    init = []
    for a in range(hps):
        init.extend(_moba_head(i, qs[a], k_ref, v_ref, km_ref[:, lanes[a]], selb_ref.at[a],
                               lanes[a], nblk))

    tile_blk = (lax.broadcasted_iota(jnp.int32, (nblk, tk), 0)
                - lax.broadcasted_iota(jnp.int32, (nblk, tk), 1) // blk)

    def body(j, carry):
        start = pl.multiple_of(j * tk, tk)
        onehot = jnp.where(tile_blk == j * kt, 1.0, 0.0).astype(BF16)
        out = []
        for a in range(hps):
            m, l, acc = carry[3 * a:3 * a + 3]
            kj = k_ref[pl.ds(start, tk), lanes[a]]
            vj = v_ref[pl.ds(start, tk), lanes[a]]
            s = (lax.dot_general(qs[a], kj, nt, preferred_element_type=F32)
                 + jnp.dot(selb_ref[a], onehot, preferred_element_type=F32))
            m_new = jnp.maximum(m, jnp.max(s, axis=1, keepdims=True))
            alpha = jnp.exp2(m - m_new)
            p = jnp.exp2(s - m_new)
            l = alpha * l + jnp.sum(p, axis=1, keepdims=True)
            acc = alpha * acc + jnp.dot(p.astype(BF16), vj, preferred_element_type=F32)
            out.extend((m_new, l, acc))
        return tuple(out)

    fin = lax.fori_loop(0, (i + kt - 1) // kt, body, tuple(init))
    for a in range(hps):
        o_ref[:, lanes[a]] = (fin[3 * a + 2] / fin[3 * a + 1]).astype(o_ref.dtype)


MOBA_HEADS_PER_STEP = 2
MOBA_BLOCKS_PER_TILE = 4


def moba_attention(qr, kr, vb, kmean, batch, seq):
    nblk = seq // MOBA_BLOCK
    hps, kt = MOBA_HEADS_PER_STEP, MOBA_BLOCKS_PER_TILE
    assert ATT_HEADS % hps == 0 and nblk % kt == 0
    w = hps * ATT_HEAD_DIM
    qspec = pl.BlockSpec((MOBA_BLOCK, w), lambda b, h, i: (b * nblk + i, h))
    kvspec = pl.BlockSpec((seq, w), lambda b, h, i: (b, h))
    return pl.pallas_call(
        functools.partial(_moba_kernel, nblk=nblk, hps=hps, kt=kt),
        out_shape=jax.ShapeDtypeStruct(qr.shape, BF16),
        grid=(batch, ATT_HEADS // hps, nblk),
        in_specs=[qspec, kvspec, kvspec,
                  pl.BlockSpec((None, nblk, w), lambda b, h, i: (b, 0, h))],
        out_specs=qspec,
        scratch_shapes=[pltpu.VMEM((hps, MOBA_BLOCK, nblk), BF16)],
        compiler_params=_cparams(("parallel", "parallel", "arbitrary")),
        name="moba_attn",
    )(qr, kr, vb, kmean)


def _silu(x):
    return x * (1.0 / (1.0 + jnp.exp(-x)))


def _ssd_kernel(z_ref, xs_ref, bc_ref, dt_ref, cwx_ref, cbx_ref, cwb_ref, cbb_ref, dtb_ref,
                alog_ref, dsk_ref, nw_ref, o_ref, xext_ref, bext_ref, state_ref, y_ref, *, heads):
    c = pl.program_id(1)
    q = SSD_CHUNK
    p = SSD_HEAD_DIM
    n = SSD_STATE
    hpg = heads // SSD_GROUPS
    gw = hpg * p

    @pl.when(c == 0)
    def _():
        xext_ref[0:HALO, :] = jnp.zeros((HALO, xext_ref.shape[1]), F32)
        bext_ref[0:HALO, :] = jnp.zeros((HALO, bext_ref.shape[1]), F32)
        state_ref[...] = jnp.zeros(state_ref.shape, F32)

    def conv_silu(ext_ref, raw, w_ref, b_ref):
        ext_ref[HALO:HALO + q, :] = raw
        acc = b_ref[...]
        for kk in range(SSD_CONV):
            off = HALO - (SSD_CONV - 1) + kk
            acc = acc + ext_ref[pl.ds(off, q), :] * w_ref[kk:kk + 1, :]
        ext_ref[0:HALO, :] = raw[q - HALO:, :]
        return _silu(acc)

    xs = conv_silu(xext_ref, xs_ref[...], cwx_ref, cbx_ref)
    bc = conv_silu(bext_ref, bc_ref[...], cwb_ref, cbb_ref)

    dt = dt_ref[...] + dtb_ref[...]
    dt = jnp.maximum(dt, 0.0) + jnp.log1p(jnp.exp(-jnp.abs(dt)))
    a = -jnp.exp(alog_ref[...])
    da = dt * a
    tri = (lax.broadcasted_iota(jnp.int32, (q, q), 0)
           >= lax.broadcasted_iota(jnp.int32, (q, q), 1))
    acs = jnp.dot(tri.astype(F32), da, preferred_element_type=F32,
                  precision=lax.Precision.HIGHEST)
    acs_t = acs.T
    a_last = acs[q - 1:q, :]
    dec_chunk = jnp.exp(a_last)
    dec_in = jnp.exp(acs)
    dec_out = jnp.exp(a_last - acs)

    for g in range(SSD_GROUPS):
        b_g = bc[:, g * n:(g + 1) * n]
        c_g = bc[:, (SSD_GROUPS + g) * n:(SSD_GROUPS + g + 1) * n]
        b_t = b_g.T.astype(BF16)
        c_b = c_g.astype(BF16)
        cb = jnp.dot(c_b, b_t, preferred_element_type=F32)
        st_g = state_ref[g]
        y_off = jnp.dot(c_b, st_g.astype(BF16), preferred_element_type=F32)
        xd_parts = []
        for j in range(hpg):
            h = g * hpg + j
            x_h = xs[:, h * p:(h + 1) * p]
            xdt = x_h * dt[:, h:h + 1]
            lmat = jnp.where(tri, jnp.exp(acs[:, h:h + 1] - acs_t[h:h + 1, :]), 0.0)
            y_d = jnp.dot((cb * lmat).astype(BF16), xdt.astype(BF16), preferred_element_type=F32)
            y_h = y_d + y_off[:, j * p:(j + 1) * p] * dec_in[:, h:h + 1] + x_h * dsk_ref[:, h:h + 1]
            y_ref[:, h * p:(h + 1) * p] = y_h
            xd_parts.append((xdt * dec_out[:, h:h + 1]).astype(BF16))
        xd = jnp.concatenate(xd_parts, axis=1)
        st_new = jnp.dot(b_t, xd, preferred_element_type=F32)
        dec_parts = [jnp.broadcast_to(dec_chunk[:, g * hpg + j:g * hpg + j + 1], (n, p))
                     for j in range(hpg)]
        state_ref[g] = st_g * jnp.concatenate(dec_parts, axis=1) + st_new

    zz = z_ref[...]
    y = y_ref[...] * _silu(zz)
    for g in range(SSD_GROUPS):
        yg = y[:, g * gw:(g + 1) * gw]
        yg = yg * lax.rsqrt(jnp.mean(yg * yg, axis=-1, keepdims=True) + EPS)
        o_ref[:, g * gw:(g + 1) * gw] = (yg * nw_ref[:, g * gw:(g + 1) * gw]).astype(o_ref.dtype)


def ssd_mixer(proj, dtp, z_col, xs_col, bc_col, dt_col, conv_w, conv_b, dt_bias, a_log, d_skip,
              norm_w, batch, seq, width):
    m = proj.shape[0]
    heads = width // SSD_HEAD_DIM
    nc = seq // SSD_CHUNK
    bcw = 2 * SSD_GROUPS * SSD_STATE
    pad = lambda v: jnp.pad(v.astype(F32), (0, LANES - heads)).reshape(1, LANES)
    row = lambda w, col: pl.BlockSpec((SSD_CHUNK, w), lambda b, c: (b * nc + c, col))
    full = lambda r, w: pl.BlockSpec((r, w), lambda b, c: (0, 0))
    return pl.pallas_call(
        functools.partial(_ssd_kernel, heads=heads),
        out_shape=jax.ShapeDtypeStruct((m, width), BF16),
        grid=(batch, nc),
        in_specs=[row(width, z_col), row(width, xs_col), row(bcw, bc_col), row(LANES, dt_col),
                  full(SSD_CONV, width), full(1, width), full(SSD_CONV, bcw), full(1, bcw),
                  full(1, LANES), full(1, LANES), full(1, LANES), full(1, width)],
        out_specs=pl.BlockSpec((SSD_CHUNK, width), lambda b, c: (b * nc + c, 0)),
        scratch_shapes=[pltpu.VMEM((HALO + SSD_CHUNK, width), F32),
                        pltpu.VMEM((HALO + SSD_CHUNK, bcw), F32),
                        pltpu.VMEM((SSD_GROUPS, SSD_STATE, width // SSD_GROUPS), F32),
                        pltpu.VMEM((SSD_CHUNK, width), F32)],
        compiler_params=_cparams(("parallel", "arbitrary")),
        name="ssd",
    )(proj, proj, proj, dtp, conv_w[:, :width], conv_b[:width].reshape(1, width),
      conv_w[:, width:], conv_b[width:].reshape(1, bcw), pad(dt_bias), pad(a_log), pad(d_skip),
      norm_w.reshape(1, width))


def _mem_attn_kernel(q_ref, k_ref, v_ref, o_ref):
    scale = MEM_HEAD_DIM ** -0.5
    nt = (((1,), (1,)), ((), ()))
    for h in range(MEM_HEADS):
        sl = slice(h * MEM_HEAD_DIM, (h + 1) * MEM_HEAD_DIM)
        q = q_ref[:, sl].astype(BF16)
        s = lax.dot_general(q, k_ref[:, sl], nt, preferred_element_type=F32) * scale
        p = jnp.exp(s - jnp.max(s, axis=1, keepdims=True))
        l = jnp.sum(p, axis=1, keepdims=True)
        o = jnp.dot(p.astype(BF16), v_ref[:, sl], preferred_element_type=F32)
        o_ref[:, sl] = (o / l).astype(o_ref.dtype)


def mem_attention(qp, kv, batch, seq, tq):
    w = MEM_HEADS * MEM_HEAD_DIM
    mem_len = kv.shape[0] // batch
    nq = seq // tq
    return pl.pallas_call(
        _mem_attn_kernel,
        out_shape=jax.ShapeDtypeStruct((batch * seq, w), BF16),
        grid=(batch, nq),
        in_specs=[pl.BlockSpec((tq, w), lambda b, i: (b * nq + i, 0)),
                  pl.BlockSpec((mem_len, w), lambda b, i: (b, 0)),
                  pl.BlockSpec((mem_len, w), lambda b, i: (b, 1))],
        out_specs=pl.BlockSpec((tq, w), lambda b, i: (b * nq + i, 0)),
        compiler_params=_cparams(("parallel", "parallel")),
        name="mem_attn",
    )(qp, kv, kv)


def _sigmoid(x):
    return 1.0 / (1.0 + jnp.exp(-x))


def _merge_kernel(ya_ref, ys_ref, ym_ref, ga_ref, gs_ref, gm_ref, wa_ref, ws_ref, wm_ref, o_ref):
    acc = _sigmoid(ga_ref[...]) * jnp.dot(ya_ref[...], wa_ref[...], preferred_element_type=F32)
    acc += _sigmoid(gs_ref[...]) * jnp.dot(ys_ref[...], ws_ref[...], preferred_element_type=F32)
    acc += _sigmoid(gm_ref[...]) * jnp.dot(ym_ref[...], wm_ref[...], preferred_element_type=F32)
    o_ref[...] = acc.astype(o_ref.dtype)


def branch_merge(ya, ys, ym, proj, gate_col, wa, ws, wm, tm, tn):
    m = ya.shape[0]
    d = wa.shape[1]
    per = d // tn
    g0 = gate_col // tn
    yspec = lambda w: pl.BlockSpec((tm, w), lambda i, j: (i, 0))
    gspec = lambda k: pl.BlockSpec((tm, tn), lambda i, j: (i, g0 + k * per + j))
    wspec = lambda w: pl.BlockSpec((w, tn), lambda i, j: (0, j))
    return pl.pallas_call(
        _merge_kernel,
        out_shape=jax.ShapeDtypeStruct((m, d), BF16),
        grid=(m // tm, per),
        in_specs=[yspec(ya.shape[1]), yspec(ys.shape[1]), yspec(ym.shape[1]),
                  gspec(0), gspec(1), gspec(2),
                  wspec(wa.shape[0]), wspec(ws.shape[0]), wspec(wm.shape[0])],
        out_specs=pl.BlockSpec((tm, tn), lambda i, j: (i, j)),
        compiler_params=_cparams(("parallel", "arbitrary")),
        name="branch_merge",
    )(ya, ys, ym, proj, proj, proj, wa, ws, wm)


def _out_residual_kernel(a_ref, w_ref, x_ref, g_ref, o_ref):
    y = jnp.dot(a_ref[...], w_ref[...], preferred_element_type=F32)
    o_ref[...] = x_ref[...] + _rms(y, g_ref[...])


def out_residual(a, w, x, gain, tm):
    m, d = x.shape
    return pl.pallas_call(
        _out_residual_kernel,
        out_shape=jax.ShapeDtypeStruct((m, d), F32),
        grid=(m // tm,),
        in_specs=[pl.BlockSpec((tm, a.shape[1]), lambda i: (i, 0)),
                  pl.BlockSpec(w.shape, lambda i: (0, 0)),
                  pl.BlockSpec((tm, d), lambda i: (i, 0)),
                  pl.BlockSpec((1, d), lambda i: (0, 0))],
        out_specs=pl.BlockSpec((tm, d), lambda i: (i, 0)),
        input_output_aliases={2: 0},
        compiler_params=_cparams(("parallel",)),
        name="out_residual",
    )(a, w, x, gain.reshape(1, d))


def _gelu_tanh(x):
    return 0.5 * x * (1.0 + jnp.tanh(0.7978845608028654 * (x + 0.044715 * x * x * x)))


def _ffn_kernel(x_ref, xh_ref, gpre_ref, wa_ref, wg_ref, cwa_ref, cwg_ref, cba_ref, cbg_ref,
                wd_ref, gpost_ref, o_ref, hn_ref, u_ref, acc_ref, *, tm, seq):
    i = pl.program_id(0)
    j = pl.program_id(1)

    @pl.when(j == 0)
    def _():
        hn_ref[0:HALO, :] = _rms(xh_ref[...], gpre_ref[...]).astype(BF16)
        hn_ref[HALO:HALO + tm, :] = _rms(x_ref[...], gpre_ref[...]).astype(BF16)
        acc_ref[...] = jnp.zeros(acc_ref.shape, F32)

    seq_start = (i * tm) % seq == 0

    def conv(w_ref, cw_ref, cb_ref):
        u = jnp.dot(hn_ref[...], w_ref[...], preferred_element_type=F32)
        u_ref[0:HALO, :] = jnp.where(seq_start, 0.0, u[0:HALO, :])
        u_ref[HALO:HALO + tm, :] = u[HALO:, :]
        out = cb_ref[...]
        for kk in range(FFN_CONV):
            off = HALO - (FFN_CONV - 1) + kk
            out = out + u_ref[pl.ds(off, tm), :] * cw_ref[kk:kk + 1, :]
        return out

    a = conv(wa_ref, cwa_ref, cba_ref)
    g = conv(wg_ref, cwg_ref, cbg_ref)
    act = (_gelu_tanh(a) * g).astype(BF16)
    acc_ref[...] += jnp.dot(act, wd_ref[...], preferred_element_type=F32)

    @pl.when(j == pl.num_programs(1) - 1)
    def _():
        o_ref[...] = x_ref[...] + _rms(acc_ref[...], gpost_ref[...])


def conv_glu_ffn(x, gpre, w_up, conv_w, conv_b, w_down, gpost, seq, tm, tf):
    m, d = x.shape
    dff = w_down.shape[0]
    nf = dff // tf
    hb = tm // HALO
    vec = lambda: pl.BlockSpec((1, d), lambda i, j: (0, 0))
    return pl.pallas_call(
        functools.partial(_ffn_kernel, tm=tm, seq=seq),
        out_shape=jax.ShapeDtypeStruct((m, d), F32),
        grid=(m // tm, nf),
        in_specs=[pl.BlockSpec((tm, d), lambda i, j: (i, 0)),
                  pl.BlockSpec((HALO, d), lambda i, j: (jnp.maximum(i * hb - 1, 0), 0)),
                  vec(),
                  pl.BlockSpec((d, tf), lambda i, j: (0, j)),
                  pl.BlockSpec((d, tf), lambda i, j: (0, nf + j)),
                  pl.BlockSpec((FFN_CONV, tf), lambda i, j: (0, j)),
                  pl.BlockSpec((FFN_CONV, tf), lambda i, j: (0, nf + j)),
                  pl.BlockSpec((1, tf), lambda i, j: (0, j)),
                  pl.BlockSpec((1, tf), lambda i, j: (0, nf + j)),
                  pl.BlockSpec((tf, d), lambda i, j: (j, 0)),
                  vec()],
        out_specs=pl.BlockSpec((tm, d), lambda i, j: (i, 0)),
        scratch_shapes=[pltpu.VMEM((HALO + tm, d), BF16),
                        pltpu.VMEM((HALO + tm, tf), F32),
                        pltpu.VMEM((tm, d), F32)],
        compiler_params=_cparams(("parallel", "arbitrary")),
        name="conv_glu_ffn",
    )(x, x, gpre.reshape(1, d), w_up, w_up, conv_w, conv_w, conv_b.reshape(1, -1),
      conv_b.reshape(1, -1), w_down, gpost.reshape(1, d))


def _rope_tables(seq):
    half = ATT_HEAD_DIM // 2
    inv = ROPE_THETA ** (-jnp.arange(half, dtype=F32) / half)
    ang = jnp.arange(seq, dtype=F32)[:, None] * inv[None, :]
    cos, sin = jnp.cos(ang), jnp.sin(ang)
    return jnp.concatenate([cos, cos], axis=1), jnp.concatenate([-sin, sin], axis=1)


def kernel(x, mem, norm_mix_pre, norm_mix_post, norm_ffn_pre, norm_ffn_post, norm_mem, w_in,
           conv_ssd_w, conv_ssd_b, dt_bias, a_log, d_skip, ssd_norm, w_mem_kv, w_br_attn, w_br_ssd,
           w_br_mem, w_out, w_up, conv_ffn_w, conv_ffn_b, w_down):
    batch, seq, d = x.shape
    depth = w_in.shape[0]
    mem_len = mem.shape[1]
    ssd_w = w_br_ssd.shape[1]
    heads = ssd_w // SSD_HEAD_DIM
    bcw = 2 * SSD_GROUPS * SSD_STATE
    mem_w = MEM_HEADS * MEM_HEAD_DIM
    assert seq % MOBA_BLOCK == 0 and seq % SSD_CHUNK == 0 and d == ssd_w

    o_q, o_z = 0, 3 * ATT_WIDTH
    o_xs = o_z + ssd_w
    o_bc = o_xs + ssd_w
    o_dt = o_bc + bcw
    o_qm = o_dt + heads
    o_g = o_qm + mem_w

    cos, sin = _rope_tables(seq)
    xf = x.reshape(batch * seq, d)
    memf = mem.reshape(batch * mem_len, d)

    for l in range(depth):
        wl = w_in[l]
        w_a = jnp.concatenate([wl[:, o_z:o_xs], wl[:, o_g:], wl[:, o_xs:o_dt], wl[:, o_q:o_z]],
                              axis=1).astype(BF16)
        w_b = jnp.concatenate([wl[:, o_qm:o_g], wl[:, o_dt:o_qm],
                               jnp.zeros((d, LANES - heads), F32)], axis=1).astype(BF16)
        proj = norm_matmul(xf, norm_mix_pre[l], w_a, F32, tm=1024, tn=512)
        projb = norm_matmul(xf, norm_mix_pre[l], w_b, F32, tm=1024, tn=w_b.shape[1])
        c_z, c_gate = 0, ssd_w
        c_xs = (4 * d) // ssd_w
        c_bc = (4 * d + ssd_w) // bcw
        c_q = (4 * d + ssd_w + bcw) // ATT_WIDTH

        qr, kr, vb, kmean = rope_kmean(proj, cos, sin, c_q, seq)
        y_a = moba_attention(qr, kr, vb, kmean.reshape(batch, seq // MOBA_BLOCK, ATT_WIDTH),
                             batch, seq)
        y_s = ssd_mixer(proj, projb, c_z, c_xs, c_bc, mem_w // LANES, conv_ssd_w[l], conv_ssd_b[l],
                        dt_bias[l], a_log[l], d_skip[l], ssd_norm[l], batch, seq, ssd_w)
        kv = norm_matmul(memf, norm_mem[l], w_mem_kv[l].astype(BF16), BF16,
                         tm=batch * mem_len, tn=512)
        y_m = mem_attention(projb, kv, batch, seq, tq=512)
        merged = branch_merge(y_a, y_s, y_m, proj, c_gate, w_br_attn[l].astype(BF16),
                              w_br_ssd[l].astype(BF16), w_br_mem[l].astype(BF16), tm=1024, tn=512)
        xf = out_residual(merged, w_out[l].astype(BF16), xf, norm_mix_post[l], tm=512)
        xf = conv_glu_ffn(xf, norm_ffn_pre[l], w_up[l].astype(BF16), conv_ffn_w[l], conv_ffn_b[l],
                          w_down[l].astype(BF16), norm_ffn_post[l], seq, tm=512, tf=512)
    return xf.reshape(batch, seq, d)
```

```python
import functools

import jax
import jax.numpy as jnp
from jax import lax
from jax.experimental import pallas as pl
from jax.experimental.pallas import tpu as pltpu

F32 = jnp.float32
BF16 = jnp.bfloat16

EPS = 1e-6
NEG = -1e30
ROPE_THETA = 10000.0

ATT_HEADS = 8
ATT_HEAD_DIM = 128
ATT_WIDTH = ATT_HEADS * ATT_HEAD_DIM
MOBA_BLOCK = 256
MOBA_TOPK = 3
LOG2E = 1.4426950408889634
Q_PRESCALE = ATT_HEAD_DIM ** -0.5 * LOG2E

SSD_HEAD_DIM = 64
SSD_GROUPS = 4
SSD_STATE = 128
SSD_CONV = 4
SSD_CHUNK = 256

MEM_HEADS = 4
MEM_HEAD_DIM = 256
FFN_CONV = 3

LANES = 128
HALO = 16
VMEM_LIMIT = 56 * 1024 * 1024


def _cparams(semantics, vmem=VMEM_LIMIT):
    return pltpu.CompilerParams(dimension_semantics=semantics, vmem_limit_bytes=vmem)


def _rms(x, gain):
    return x * lax.rsqrt(jnp.mean(x * x, axis=-1, keepdims=True) + EPS) * gain


def _norm_matmul_kernel(x_ref, g_ref, w_ref, o_ref, hn_ref):
    @pl.when(pl.program_id(1) == 0)
    def _():
        hn_ref[...] = _rms(x_ref[...], g_ref[...]).astype(BF16)

    o_ref[...] = jnp.dot(hn_ref[...], w_ref[...], preferred_element_type=F32).astype(o_ref.dtype)


def norm_matmul(x, gain, w, out_dtype, tm, tn):
    m, d = x.shape
    n = w.shape[1]
    return pl.pallas_call(
        _norm_matmul_kernel,
        out_shape=jax.ShapeDtypeStruct((m, n), out_dtype),
        grid=(m // tm, n // tn),
        in_specs=[pl.BlockSpec((tm, d), lambda i, j: (i, 0)),
                  pl.BlockSpec((1, d), lambda i, j: (0, 0)),
                  pl.BlockSpec((d, tn), lambda i, j: (0, j))],
        out_specs=pl.BlockSpec((tm, tn), lambda i, j: (i, j)),
        scratch_shapes=[pltpu.VMEM((tm, d), BF16)],
        compiler_params=_cparams(("parallel", "arbitrary")),
        name="norm_matmul",
    )(x, gain.reshape(1, d), w)


def _rope_kernel(q_ref, k_ref, v_ref, cos_ref, sin_ref, qo_ref, ka_ref, vt_ref, km_ref, *, per_seq):
    cos = cos_ref[...]
    sin = sin_ref[...]
    half = ATT_HEAD_DIM // 2
    dh = ATT_HEAD_DIM
    blk_idx = pl.program_id(0) % per_seq
    lane = lax.broadcasted_iota(jnp.int32, (MOBA_BLOCK, dh), 1)
    onehot = jnp.where(lane == blk_idx, 1.0, 0.0).astype(BF16)
    for h in range(ATT_HEADS):
        sl = slice(h * dh, (h + 1) * dh)
        q = q_ref[:, sl]
        k = k_ref[:, sl]
        qr = q * cos + pltpu.roll(q, half, 1) * sin
        kr = k * cos + pltpu.roll(k, half, 1) * sin
        qo_ref[:, sl] = (qr * Q_PRESCALE).astype(BF16)
        ka_ref[:, 2 * h * dh:(2 * h + 1) * dh] = kr.astype(BF16)
        ka_ref[:, (2 * h + 1) * dh:(2 * h + 2) * dh] = onehot
        km_ref[0, :, sl] = jnp.mean(kr, axis=0, keepdims=True)
        vt_ref[0, h] = v_ref[:, sl].T.astype(BF16)


def rope_kmean(proj, cos, sin, q_col, seq):
    m = proj.shape[0]
    nblk = m // MOBA_BLOCK
    per_seq = seq // MOBA_BLOCK
    assert per_seq <= ATT_HEAD_DIM
    row = lambda c: pl.BlockSpec((MOBA_BLOCK, ATT_WIDTH), lambda i: (i, c))
    tab = pl.BlockSpec((MOBA_BLOCK, ATT_HEAD_DIM), lambda i: (i % per_seq, 0))
    return pl.pallas_call(
        functools.partial(_rope_kernel, per_seq=per_seq),
        out_shape=(jax.ShapeDtypeStruct((m, ATT_WIDTH), BF16),
                   jax.ShapeDtypeStruct((m, 2 * ATT_WIDTH), BF16),
                   jax.ShapeDtypeStruct((nblk, ATT_HEADS, ATT_HEAD_DIM, MOBA_BLOCK), BF16),
                   jax.ShapeDtypeStruct((nblk, 1, ATT_WIDTH), F32)),
        grid=(nblk,),
        in_specs=[row(q_col), row(q_col + 1), row(q_col + 2), tab, tab],
        out_specs=(pl.BlockSpec((MOBA_BLOCK, ATT_WIDTH), lambda i: (i, 0)),
                   pl.BlockSpec((MOBA_BLOCK, 2 * ATT_WIDTH), lambda i: (i, 0)),
                   pl.BlockSpec((1, ATT_HEADS, ATT_HEAD_DIM, MOBA_BLOCK), lambda i: (i, 0, 0, 0)),
                   pl.BlockSpec((1, 1, ATT_WIDTH), lambda i: (i, 0, 0))),
        compiler_params=_cparams(("parallel",)),
        name="rope_kmean",
    )(proj, proj, proj, cos, sin)


def _moba_head(i, q, ka_ref, vt_ref, km, qa_ref, a, nblk):
    blk = MOBA_BLOCK
    dh = ATT_HEAD_DIM
    nt = (((1,), (1,)), ((), ()))

    km_hi = km.astype(BF16)
    r1 = km - km_hi.astype(F32)
    km_mid = r1.astype(BF16)
    km_lo = (r1 - km_mid.astype(F32)).astype(BF16)
    gate = (lax.dot_general(km_hi, q, nt, preferred_element_type=F32)
            + lax.dot_general(km_mid, q, nt, preferred_element_type=F32)
            + lax.dot_general(km_lo, q, nt, preferred_element_type=F32))
    row = lax.broadcasted_iota(jnp.int32, (nblk, blk), 0).astype(F32)
    past = row < i.astype(F32)
    g = jnp.where(past, gate, NEG)
    bias = jnp.full((nblk, blk), NEG, F32)
    for _ in range(min(MOBA_TOPK, max(nblk - 1, 1))):
        mx = jnp.max(g, axis=0, keepdims=True)
        first = jnp.min(jnp.where(g == mx, row, float(nblk)), axis=0, keepdims=True)
        pick = row == first
        bias = jnp.where(pick, 0.0, bias)
        g = jnp.where(pick, -jnp.inf, g)
    bias = jnp.where(past, bias, NEG)
    bias_t = jnp.concatenate([bias, jnp.zeros((dh - nblk, blk), F32)], axis=0).T
    qa_ref[a, :, 0:dh] = q
    qa_ref[a, :, dh:2 * dh] = bias_t.astype(BF16)

    start = pl.multiple_of(i * blk, blk)
    k_own = ka_ref[pl.ds(start, blk), 2 * a * dh:(2 * a + 1) * dh]
    s = lax.dot_general(k_own, q, nt, preferred_element_type=F32)
    key = lax.broadcasted_iota(jnp.int32, (blk, blk), 0)
    qry = lax.broadcasted_iota(jnp.int32, (blk, blk), 1)
    s = jnp.where(key <= qry, s, NEG)
    m0 = jnp.max(s, axis=0, keepdims=True)
    p = jnp.exp2(s - m0)
    l0 = jnp.sum(p, axis=0, keepdims=True)
    acc0 = jnp.dot(vt_ref[i, a], p.astype(BF16), preferred_element_type=F32)
    return m0, l0, acc0


def _moba_kernel(q_ref, ka_ref, vt_ref, km_ref, o_ref, qa_ref, s_ref, *, nblk, hps, kt):
    i = pl.program_id(2)
    blk = MOBA_BLOCK
    dh = ATT_HEAD_DIM
    tk = kt * blk
    last_tile = nblk // kt - 1
    nt = (((1,), (1,)), ((), ()))
    init = []
    for a in range(hps):
        lanes = slice(a * dh, (a + 1) * dh)
        init.extend(_moba_head(i, q_ref[:, lanes], ka_ref, vt_ref, km_ref[:, lanes], qa_ref, a, nblk))

    def scores_into(slot, t):
        start = pl.multiple_of(jnp.minimum(t, last_tile) * tk, tk)
        for a in range(hps):
            s_ref[slot, a] = lax.dot_general(ka_ref[pl.ds(start, tk), 2 * a * dh:(2 * a + 2) * dh],
                                             qa_ref[a], nt, preferred_element_type=F32)

    def update(slot, t, state):
        for a in range(hps):
            m, l, acc = state[3 * a:3 * a + 3]
            s = s_ref[slot, a]
            m_new = jnp.maximum(m, jnp.max(s, axis=0, keepdims=True))
            alpha = jnp.exp2(m - m_new)
            p = jnp.exp2(s - m_new)
            l = alpha * l + jnp.sum(p, axis=0, keepdims=True)
            pb = p.astype(BF16)
            acc = alpha * acc
            for c in range(kt):
                acc = acc + jnp.dot(vt_ref[t * kt + c, a], pb[c * blk:(c + 1) * blk, :],
                                    preferred_element_type=F32)
            state[3 * a:3 * a + 3] = [m_new, l, acc]
        return state

    scores_into(0, 0)

    def body(jj, carry):
        state = list(carry)
        scores_into(1, 2 * jj + 1)
        state = update(0, 2 * jj, state)
        scores_into(0, 2 * jj + 2)
        state = update(1, jnp.minimum(2 * jj + 1, last_tile), state)
        return tuple(state)

    ntile = (i + kt - 1) // kt
    fin = lax.fori_loop(0, (ntile + 1) // 2, body, tuple(init))
    for a in range(hps):
        o_t = fin[3 * a + 2] / fin[3 * a + 1]
        o_ref[:, a * dh:(a + 1) * dh] = o_t.T.astype(o_ref.dtype)


MOBA_HEADS_PER_STEP = 2
MOBA_BLOCKS_PER_TILE = 2


def moba_attention(qr, ka, vt, kmean, batch, seq):
    nblk = seq // MOBA_BLOCK
    hps, kt = MOBA_HEADS_PER_STEP, MOBA_BLOCKS_PER_TILE
    assert ATT_HEADS % hps == 0 and nblk % kt == 0 and nblk <= ATT_HEAD_DIM
    w = hps * ATT_HEAD_DIM
    qspec = pl.BlockSpec((MOBA_BLOCK, w), lambda b, h, i: (b * nblk + i, h))
    return pl.pallas_call(
        functools.partial(_moba_kernel, nblk=nblk, hps=hps, kt=kt),
        out_shape=jax.ShapeDtypeStruct(qr.shape, BF16),
        grid=(batch, ATT_HEADS // hps, nblk),
        in_specs=[qspec,
                  pl.BlockSpec((seq, 2 * w), lambda b, h, i: (b, h)),
                  pl.BlockSpec((nblk, hps, ATT_HEAD_DIM, MOBA_BLOCK), lambda b, h, i: (b, h, 0, 0)),
                  pl.BlockSpec((None, nblk, w), lambda b, h, i: (b, 0, h))],
        out_specs=qspec,
        scratch_shapes=[pltpu.VMEM((hps, MOBA_BLOCK, 2 * ATT_HEAD_DIM), BF16),
                        pltpu.VMEM((2, hps, kt * MOBA_BLOCK, MOBA_BLOCK), F32)],
        compiler_params=_cparams(("parallel", "parallel", "arbitrary")),
        name="moba_attn",
    )(qr, ka, vt, kmean)


def _silu(x):
    return x * (0.5 * jnp.tanh(0.5 * x) + 0.5)


def _ssd_kernel(z_ref, xs_ref, bc_ref, dt_ref, cwx_ref, cbx_ref, cwb_ref, cbb_ref, dtb_ref,
                alog_ref, dsk_ref, nw_ref, e3_ref, o_ref, xext_ref, bext_ref, state_ref, y_ref,
                *, heads):
    c = pl.program_id(1)
    q = SSD_CHUNK
    p = SSD_HEAD_DIM
    n = SSD_STATE
    hpg = heads // SSD_GROUPS
    gw = hpg * p

    @pl.when(c == 0)
    def _():
        xext_ref[0:HALO, :] = jnp.zeros((HALO, xext_ref.shape[1]), F32)
        bext_ref[0:HALO, :] = jnp.zeros((HALO, bext_ref.shape[1]), F32)
        state_ref[...] = jnp.zeros(state_ref.shape, F32)

    def conv_silu(ext_ref, raw, w_ref, b_ref):
        ext_ref[HALO:HALO + q, :] = raw
        acc = b_ref[...]
        for kk in range(SSD_CONV):
            off = HALO - (SSD_CONV - 1) + kk
            acc = acc + ext_ref[pl.ds(off, q), :] * w_ref[kk:kk + 1, :]
        ext_ref[0:HALO, :] = raw[q - HALO:, :]
        return _silu(acc)

    xs = conv_silu(xext_ref, xs_ref[...], cwx_ref, cbx_ref)
    bc = conv_silu(bext_ref, bc_ref[...], cwb_ref, cbb_ref)

    dt = dt_ref[...] + dtb_ref[...]
    dt = jnp.maximum(dt, 0.0) + jnp.log1p(jnp.exp(-jnp.abs(dt)))
    a = -jnp.exp(alog_ref[...])
    da = dt * a
    tri = (lax.broadcasted_iota(jnp.int32, (q, q), 0)
           >= lax.broadcasted_iota(jnp.int32, (q, q), 1))
    acs = jnp.dot(tri.astype(F32), da, preferred_element_type=F32,
                  precision=lax.Precision.HIGHEST)
    acs2 = acs * LOG2E
    acs2_t = acs2.T
    hq = q // 2
    tri_h = tri[:hq, :hq]
    a_last = acs[q - 1:q, :]

    def per_channel(v):
        hi = v.astype(BF16)
        r1 = v - hi.astype(F32)
        mid = r1.astype(BF16)
        lo = (r1 - mid.astype(F32)).astype(BF16)
        return jnp.dot(jnp.concatenate([hi, mid, lo], axis=1), e3_ref[...],
                       preferred_element_type=F32)

    dec_in = per_channel(jnp.exp(acs))
    xdt = xs * per_channel(dt)
    xd = (xs * per_channel(dt * jnp.exp(a_last - acs))).astype(BF16)
    skip = xs * dsk_ref[...]
    first_head = lax.broadcasted_iota(jnp.int32, (q, 2 * p), 1) < p

    def decay_quadrants(cb, h):
        col, rw = acs2[:, h:h + 1], acs2_t[h:h + 1, :]
        m00 = cb[:hq, :hq] * jnp.where(tri_h, jnp.exp2(col[:hq] - rw[:, :hq]), 0.0)
        m10 = cb[hq:, :hq] * jnp.exp2(col[hq:] - rw[:, :hq])
        m11 = cb[hq:, hq:] * jnp.where(tri_h, jnp.exp2(col[hq:] - rw[:, hq:]), 0.0)
        return m00.astype(BF16), jnp.concatenate([m10, m11], axis=1).astype(BF16)

    for g in range(SSD_GROUPS):
        b_g = bc[:, g * n:(g + 1) * n]
        c_g = bc[:, (SSD_GROUPS + g) * n:(SSD_GROUPS + g + 1) * n]
        b_t = b_g.T.astype(BF16)
        c_b = c_g.astype(BF16)
        cb = jnp.dot(c_b, b_t, preferred_element_type=F32)
        gl = slice(g * gw, (g + 1) * gw)
        st_g = state_ref[g]
        y_off = jnp.dot(c_b, st_g.astype(BF16), preferred_element_type=F32)
        for j in range(0, hpg, 2):
            h = g * hpg + j
            pl_ = slice(h * p, (h + 2) * p)
            xp = xdt[:, pl_]
            xa = jnp.where(first_head, xp, 0.0).astype(BF16)
            xb = jnp.where(first_head, 0.0, xp).astype(BF16)
            top_a, bot_a = decay_quadrants(cb, h)
            top_b, bot_b = decay_quadrants(cb, h + 1)
            y_top = (jnp.dot(top_a, xa[:hq], preferred_element_type=F32)
                     + jnp.dot(top_b, xb[:hq], preferred_element_type=F32))
            y_bot = (jnp.dot(bot_a, xa, preferred_element_type=F32)
                     + jnp.dot(bot_b, xb, preferred_element_type=F32))
            y_ref[:, pl_] = (jnp.concatenate([y_top, y_bot], axis=0)
                             + y_off[:, j * p:(j + 2) * p] * dec_in[:, pl_] + skip[:, pl_])
        st_new = jnp.dot(b_t, xd[:, gl], preferred_element_type=F32)
        state_ref[g] = st_g * dec_in[q - 1:q, gl] + st_new

    zz = z_ref[...]
    y = y_ref[...] * _silu(zz)
    for g in range(SSD_GROUPS):
        yg = y[:, g * gw:(g + 1) * gw]
        yg = yg * lax.rsqrt(jnp.mean(yg * yg, axis=-1, keepdims=True) + EPS)
        o_ref[:, g * gw:(g + 1) * gw] = (yg * nw_ref[:, g * gw:(g + 1) * gw]).astype(o_ref.dtype)


def ssd_mixer(proj, dtp, z_col, xs_col, bc_col, dt_col, conv_w, conv_b, dt_bias, a_log, d_skip,
              norm_w, batch, seq, width):
    m = proj.shape[0]
    heads = width // SSD_HEAD_DIM
    nc = seq // SSD_CHUNK
    bcw = 2 * SSD_GROUPS * SSD_STATE
    assert heads <= LANES and (heads // SSD_GROUPS) % 2 == 0 and 2 * SSD_HEAD_DIM == LANES
    pad = lambda v: jnp.pad(v.astype(F32), (0, LANES - heads)).reshape(1, LANES)
    row = lambda w, col: pl.BlockSpec((SSD_CHUNK, w), lambda b, c: (b * nc + c, col))
    full = lambda r, w: pl.BlockSpec((r, w), lambda b, c: (0, 0))
    head_of = jnp.arange(width, dtype=jnp.int32) // SSD_HEAD_DIM
    e1 = (jnp.arange(LANES, dtype=jnp.int32)[:, None] == head_of[None, :]).astype(BF16)
    e3 = jnp.concatenate([e1, e1, e1], axis=0)
    return pl.pallas_call(
        functools.partial(_ssd_kernel, heads=heads),
        out_shape=jax.ShapeDtypeStruct((m, width), BF16),
        grid=(batch, nc),
        in_specs=[row(width, z_col), row(width, xs_col), row(bcw, bc_col), row(LANES, dt_col),
                  full(SSD_CONV, width), full(1, width), full(SSD_CONV, bcw), full(1, bcw),
                  full(1, LANES), full(1, LANES), full(1, width), full(1, width),
                  full(3 * LANES, width)],
        out_specs=pl.BlockSpec((SSD_CHUNK, width), lambda b, c: (b * nc + c, 0)),
        scratch_shapes=[pltpu.VMEM((HALO + SSD_CHUNK, width), F32),
                        pltpu.VMEM((HALO + SSD_CHUNK, bcw), F32),
                        pltpu.VMEM((SSD_GROUPS, SSD_STATE, width // SSD_GROUPS), F32),
                        pltpu.VMEM((SSD_CHUNK, width), F32)],
        compiler_params=_cparams(("parallel", "arbitrary")),
        name="ssd",
    )(proj, proj, proj, dtp, conv_w[:, :width], conv_b[:width].reshape(1, width),
      conv_w[:, width:], conv_b[width:].reshape(1, bcw), pad(dt_bias), pad(a_log),
      jnp.repeat(d_skip.astype(F32), SSD_HEAD_DIM).reshape(1, width), norm_w.reshape(1, width), e3)


def _mem_attn_kernel(q_ref, k_ref, v_ref, o_ref):
    scale = MEM_HEAD_DIM ** -0.5
    nt = (((1,), (1,)), ((), ()))
    for h in range(MEM_HEADS):
        sl = slice(h * MEM_HEAD_DIM, (h + 1) * MEM_HEAD_DIM)
        q = q_ref[:, sl].astype(BF16)
        s = lax.dot_general(q, k_ref[:, sl], nt, preferred_element_type=F32) * scale
        p = jnp.exp(s - jnp.max(s, axis=1, keepdims=True))
        l = jnp.sum(p, axis=1, keepdims=True)
        o = jnp.dot(p.astype(BF16), v_ref[:, sl], preferred_element_type=F32)
        o_ref[:, sl] = (o / l).astype(o_ref.dtype)


def mem_attention(qp, kv, batch, seq, tq):
    w = MEM_HEADS * MEM_HEAD_DIM
    mem_len = kv.shape[0] // batch
    nq = seq // tq
    return pl.pallas_call(
        _mem_attn_kernel,
        out_shape=jax.ShapeDtypeStruct((batch * seq, w), BF16),
        grid=(batch, nq),
        in_specs=[pl.BlockSpec((tq, w), lambda b, i: (b * nq + i, 0)),
                  pl.BlockSpec((mem_len, w), lambda b, i: (b, 0)),
                  pl.BlockSpec((mem_len, w), lambda b, i: (b, 1))],
        out_specs=pl.BlockSpec((tq, w), lambda b, i: (b * nq + i, 0)),
        compiler_params=_cparams(("parallel", "parallel")),
        name="mem_attn",
    )(qp, kv, kv)


def _sigmoid(x):
    return 1.0 / (1.0 + jnp.exp(-x))


def _merge_kernel(ya_ref, ys_ref, ym_ref, ga_ref, gs_ref, gm_ref, wa_ref, ws_ref, wm_ref, o_ref):
    acc = _sigmoid(ga_ref[...]) * jnp.dot(ya_ref[...], wa_ref[...], preferred_element_type=F32)
    acc += _sigmoid(gs_ref[...]) * jnp.dot(ys_ref[...], ws_ref[...], preferred_element_type=F32)
    acc += _sigmoid(gm_ref[...]) * jnp.dot(ym_ref[...], wm_ref[...], preferred_element_type=F32)
    o_ref[...] = acc.astype(o_ref.dtype)


def branch_merge(ya, ys, ym, proj, gate_col, wa, ws, wm, tm, tn):
    m = ya.shape[0]
    d = wa.shape[1]
    per = d // tn
    g0 = gate_col // tn
    yspec = lambda w: pl.BlockSpec((tm, w), lambda i, j: (i, 0))
    gspec = lambda k: pl.BlockSpec((tm, tn), lambda i, j: (i, g0 + k * per + j))
    wspec = lambda w: pl.BlockSpec((w, tn), lambda i, j: (0, j))
    return pl.pallas_call(
        _merge_kernel,
        out_shape=jax.ShapeDtypeStruct((m, d), BF16),
        grid=(m // tm, per),
        in_specs=[yspec(ya.shape[1]), yspec(ys.shape[1]), yspec(ym.shape[1]),
                  gspec(0), gspec(1), gspec(2),
                  wspec(wa.shape[0]), wspec(ws.shape[0]), wspec(wm.shape[0])],
        out_specs=pl.BlockSpec((tm, tn), lambda i, j: (i, j)),
        compiler_params=_cparams(("parallel", "arbitrary")),
        name="branch_merge",
    )(ya, ys, ym, proj, proj, proj, wa, ws, wm)


def _out_residual_kernel(a_ref, w_ref, x_ref, g_ref, o_ref):
    y = jnp.dot(a_ref[...], w_ref[...], preferred_element_type=F32)
    o_ref[...] = x_ref[...] + _rms(y, g_ref[...])


def out_residual(a, w, x, gain, tm):
    m, d = x.shape
    return pl.pallas_call(
        _out_residual_kernel,
        out_shape=jax.ShapeDtypeStruct((m, d), F32),
        grid=(m // tm,),
        in_specs=[pl.BlockSpec((tm, a.shape[1]), lambda i: (i, 0)),
                  pl.BlockSpec(w.shape, lambda i: (0, 0)),
                  pl.BlockSpec((tm, d), lambda i: (i, 0)),
                  pl.BlockSpec((1, d), lambda i: (0, 0))],
        out_specs=pl.BlockSpec((tm, d), lambda i: (i, 0)),
        input_output_aliases={2: 0},
        compiler_params=_cparams(("parallel",)),
        name="out_residual",
    )(a, w, x, gain.reshape(1, d))


def _gelu_tanh(x):
    return 0.5 * x * (1.0 + jnp.tanh(0.7978845608028654 * (x + 0.044715 * x * x * x)))


def _ffn_kernel(x_ref, xh_ref, gpre_ref, wa_ref, wg_ref, cwa_ref, cwg_ref, cba_ref, cbg_ref,
                wd_ref, gpost_ref, o_ref, hn_ref, u_ref, acc_ref, *, tm, seq):
    i = pl.program_id(0)
    j = pl.program_id(1)

    @pl.when(j == 0)
    def _():
        hn_ref[0:HALO, :] = _rms(xh_ref[...], gpre_ref[...]).astype(BF16)
        hn_ref[HALO:HALO + tm, :] = _rms(x_ref[...], gpre_ref[...]).astype(BF16)
        acc_ref[...] = jnp.zeros(acc_ref.shape, F32)

    seq_start = (i * tm) % seq == 0

    def conv(w_ref, cw_ref, cb_ref):
        u = jnp.dot(hn_ref[...], w_ref[...], preferred_element_type=F32)
        u_ref[0:HALO, :] = jnp.where(seq_start, 0.0, u[0:HALO, :])
        u_ref[HALO:HALO + tm, :] = u[HALO:, :]
        out = cb_ref[...]
        for kk in range(FFN_CONV):
            off = HALO - (FFN_CONV - 1) + kk
            out = out + u_ref[pl.ds(off, tm), :] * cw_ref[kk:kk + 1, :]
        return out

    a = conv(wa_ref, cwa_ref, cba_ref)
    g = conv(wg_ref, cwg_ref, cbg_ref)
    act = (_gelu_tanh(a) * g).astype(BF16)
    acc_ref[...] += jnp.dot(act, wd_ref[...], preferred_element_type=F32)

    @pl.when(j == pl.num_programs(1) - 1)
    def _():
        o_ref[...] = x_ref[...] + _rms(acc_ref[...], gpost_ref[...])


def conv_glu_ffn(x, gpre, w_up, conv_w, conv_b, w_down, gpost, seq, tm, tf):
    m, d = x.shape
    dff = w_down.shape[0]
    nf = dff // tf
    hb = tm // HALO
    vec = lambda: pl.BlockSpec((1, d), lambda i, j: (0, 0))
    return pl.pallas_call(
        functools.partial(_ffn_kernel, tm=tm, seq=seq),
        out_shape=jax.ShapeDtypeStruct((m, d), F32),
        grid=(m // tm, nf),
        in_specs=[pl.BlockSpec((tm, d), lambda i, j: (i, 0)),
                  pl.BlockSpec((HALO, d), lambda i, j: (jnp.maximum(i * hb - 1, 0), 0)),
                  vec(),
                  pl.BlockSpec((d, tf), lambda i, j: (0, j)),
                  pl.BlockSpec((d, tf), lambda i, j: (0, nf + j)),
                  pl.BlockSpec((FFN_CONV, tf), lambda i, j: (0, j)),
                  pl.BlockSpec((FFN_CONV, tf), lambda i, j: (0, nf + j)),
                  pl.BlockSpec((1, tf), lambda i, j: (0, j)),
                  pl.BlockSpec((1, tf), lambda i, j: (0, nf + j)),
                  pl.BlockSpec((tf, d), lambda i, j: (j, 0)),
                  vec()],
        out_specs=pl.BlockSpec((tm, d), lambda i, j: (i, 0)),
        scratch_shapes=[pltpu.VMEM((HALO + tm, d), BF16),
                        pltpu.VMEM((HALO + tm, tf), F32),
                        pltpu.VMEM((tm, d), F32)],
        compiler_params=_cparams(("parallel", "arbitrary")),
        name="conv_glu_ffn",
    )(x, x, gpre.reshape(1, d), w_up, w_up, conv_w, conv_w, conv_b.reshape(1, -1),
      conv_b.reshape(1, -1), w_down, gpost.reshape(1, d))


def _rope_tables(seq):
    half = ATT_HEAD_DIM // 2
    inv = ROPE_THETA ** (-jnp.arange(half, dtype=F32) / half)
    ang = jnp.arange(seq, dtype=F32)[:, None] * inv[None, :]
    cos, sin = jnp.cos(ang), jnp.sin(ang)
    return jnp.concatenate([cos, cos], axis=1), jnp.concatenate([-sin, sin], axis=1)


def kernel(x, mem, norm_mix_pre, norm_mix_post, norm_ffn_pre, norm_ffn_post, norm_mem, w_in,
           conv_ssd_w, conv_ssd_b, dt_bias, a_log, d_skip, ssd_norm, w_mem_kv, w_br_attn, w_br_ssd,
           w_br_mem, w_out, w_up, conv_ffn_w, conv_ffn_b, w_down):
    batch, seq, d = x.shape
    depth = w_in.shape[0]
    mem_len = mem.shape[1]
    ssd_w = w_br_ssd.shape[1]
    heads = ssd_w // SSD_HEAD_DIM
    bcw = 2 * SSD_GROUPS * SSD_STATE
    mem_w = MEM_HEADS * MEM_HEAD_DIM
    assert seq % MOBA_BLOCK == 0 and seq % SSD_CHUNK == 0 and d == ssd_w

    o_q, o_z = 0, 3 * ATT_WIDTH
    o_xs = o_z + ssd_w
    o_bc = o_xs + ssd_w
    o_dt = o_bc + bcw
    o_qm = o_dt + heads
    o_g = o_qm + mem_w

    cos, sin = _rope_tables(seq)
    xf = x.reshape(batch * seq, d)
    memf = mem.reshape(batch * mem_len, d)

    for l in range(depth):
        wl = w_in[l]
        w_a = jnp.concatenate([wl[:, o_z:o_xs], wl[:, o_g:], wl[:, o_xs:o_dt], wl[:, o_q:o_z]],
                              axis=1).astype(BF16)
        w_b = jnp.concatenate([wl[:, o_qm:o_g], wl[:, o_dt:o_qm],
                               jnp.zeros((d, LANES - heads), F32)], axis=1).astype(BF16)
        proj = norm_matmul(xf, norm_mix_pre[l], w_a, F32, tm=1024, tn=512)
        projb = norm_matmul(xf, norm_mix_pre[l], w_b, F32, tm=1024, tn=w_b.shape[1])
        c_z, c_gate = 0, ssd_w
        c_xs = (4 * d) // ssd_w
        c_bc = (4 * d + ssd_w) // bcw
        c_q = (4 * d + ssd_w + bcw) // ATT_WIDTH

        qr, kr, vb, kmean = rope_kmean(proj, cos, sin, c_q, seq)
        y_a = moba_attention(qr, kr, vb, kmean.reshape(batch, seq // MOBA_BLOCK, ATT_WIDTH),
                             batch, seq)
        y_s = ssd_mixer(proj, projb, c_z, c_xs, c_bc, mem_w // LANES, conv_ssd_w[l], conv_ssd_b[l],
                        dt_bias[l], a_log[l], d_skip[l], ssd_norm[l], batch, seq, ssd_w)
        kv = norm_matmul(memf, norm_mem[l], w_mem_kv[l].astype(BF16), BF16,
                         tm=batch * mem_len, tn=512)
        y_m = mem_attention(projb, kv, batch, seq, tq=512)
        merged = branch_merge(y_a, y_s, y_m, proj, c_gate, w_br_attn[l].astype(BF16),
                              w_br_ssd[l].astype(BF16), w_br_mem[l].astype(BF16), tm=1024, tn=512)
        xf = out_residual(merged, w_out[l].astype(BF16), xf, norm_mix_post[l], tm=512)
        xf = conv_glu_ffn(xf, norm_ffn_pre[l], w_up[l].astype(BF16), conv_ffn_w[l], conv_ffn_b[l],
                          w_down[l].astype(BF16), norm_ffn_post[l], seq, tm=512, tf=512)
    return xf.reshape(batch, seq, d)
```

```python
import functools

import jax
import jax.numpy as jnp
from jax import lax
from jax.experimental import pallas as pl
from jax.experimental.pallas import tpu as pltpu

F32 = jnp.float32
BF16 = jnp.bfloat16

EPS = 1e-6
NEG = -1e30
ROPE_THETA = 10000.0

ATT_HEADS = 8
ATT_HEAD_DIM = 128
ATT_WIDTH = ATT_HEADS * ATT_HEAD_DIM
MOBA_BLOCK = 256
MOBA_TOPK = 3
LOG2E = 1.4426950408889634
Q_PRESCALE = ATT_HEAD_DIM ** -0.5 * LOG2E

SSD_HEAD_DIM = 64
SSD_GROUPS = 4
SSD_STATE = 128
SSD_CONV = 4
SSD_CHUNK = 256

MEM_HEADS = 4
MEM_HEAD_DIM = 256
FFN_CONV = 3

LANES = 128
HALO = 16
VMEM_LIMIT = 56 * 1024 * 1024


def _cparams(semantics, vmem=VMEM_LIMIT):
    return pltpu.CompilerParams(dimension_semantics=semantics, vmem_limit_bytes=vmem)


def _rms(x, gain):
    return x * lax.rsqrt(jnp.mean(x * x, axis=-1, keepdims=True) + EPS) * gain


def _norm_matmul_kernel(x_ref, g_ref, w_ref, o_ref, hn_ref):
    @pl.when(pl.program_id(1) == 0)
    def _():
        hn_ref[...] = _rms(x_ref[...], g_ref[...]).astype(BF16)

    o_ref[...] = jnp.dot(hn_ref[...], w_ref[...], preferred_element_type=F32).astype(o_ref.dtype)


def _wspec(w, layer, block, index):
    if w.ndim == 2:
        return pl.BlockSpec(block, index)
    return pl.BlockSpec((None,) + tuple(block), lambda *g: (layer,) + tuple(index(*g)))


def norm_matmul(x, gain, w, out_dtype, tm, tn, layer=None):
    m, d = x.shape
    n = w.shape[-1]
    return pl.pallas_call(
        _norm_matmul_kernel,
        out_shape=jax.ShapeDtypeStruct((m, n), out_dtype),
        grid=(m // tm, n // tn),
        in_specs=[pl.BlockSpec((tm, d), lambda i, j: (i, 0)),
                  pl.BlockSpec((1, d), lambda i, j: (0, 0)),
                  _wspec(w, layer, (d, tn), lambda i, j: (0, j))],
        out_specs=pl.BlockSpec((tm, tn), lambda i, j: (i, j)),
        scratch_shapes=[pltpu.VMEM((tm, d), BF16)],
        compiler_params=_cparams(("parallel", "arbitrary")),
        name="norm_matmul",
    )(x, gain.reshape(1, d), w)


def _rope_kernel(q_ref, k_ref, v_ref, cos_ref, sin_ref, qo_ref, ka_ref, vt_ref, km_ref, *, per_seq):
    cos = cos_ref[...]
    sin = sin_ref[...]
    half = ATT_HEAD_DIM // 2
    dh = ATT_HEAD_DIM
    blk_idx = pl.program_id(0) % per_seq
    lane = lax.broadcasted_iota(jnp.int32, (MOBA_BLOCK, dh), 1)
    onehot = jnp.where(lane == blk_idx, 1.0, 0.0).astype(BF16)
    for h in range(ATT_HEADS):
        sl = slice(h * dh, (h + 1) * dh)
        q = q_ref[:, sl]
        k = k_ref[:, sl]
        qr = q * cos + pltpu.roll(q, half, 1) * sin
        kr = k * cos + pltpu.roll(k, half, 1) * sin
        qo_ref[:, sl] = (qr * Q_PRESCALE).astype(BF16)
        ka_ref[:, 2 * h * dh:(2 * h + 1) * dh] = kr.astype(BF16)
        ka_ref[:, (2 * h + 1) * dh:(2 * h + 2) * dh] = onehot
        km_ref[0, :, sl] = jnp.mean(kr, axis=0, keepdims=True)
        vt_ref[0, h] = v_ref[:, sl].T.astype(BF16)


def rope_kmean(proj, cos, sin, q_col, seq):
    m = proj.shape[0]
    nblk = m // MOBA_BLOCK
    per_seq = seq // MOBA_BLOCK
    assert per_seq <= ATT_HEAD_DIM
    row = lambda c: pl.BlockSpec((MOBA_BLOCK, ATT_WIDTH), lambda i: (i, c))
    tab = pl.BlockSpec((MOBA_BLOCK, ATT_HEAD_DIM), lambda i: (i % per_seq, 0))
    return pl.pallas_call(
        functools.partial(_rope_kernel, per_seq=per_seq),
        out_shape=(jax.ShapeDtypeStruct((m, ATT_WIDTH), BF16),
                   jax.ShapeDtypeStruct((m, 2 * ATT_WIDTH), BF16),
                   jax.ShapeDtypeStruct((nblk, ATT_HEADS, ATT_HEAD_DIM, MOBA_BLOCK), BF16),
                   jax.ShapeDtypeStruct((nblk, 1, ATT_WIDTH), F32)),
        grid=(nblk,),
        in_specs=[row(q_col), row(q_col + 1), row(q_col + 2), tab, tab],
        out_specs=(pl.BlockSpec((MOBA_BLOCK, ATT_WIDTH), lambda i: (i, 0)),
                   pl.BlockSpec((MOBA_BLOCK, 2 * ATT_WIDTH), lambda i: (i, 0)),
                   pl.BlockSpec((1, ATT_HEADS, ATT_HEAD_DIM, MOBA_BLOCK), lambda i: (i, 0, 0, 0)),
                   pl.BlockSpec((1, 1, ATT_WIDTH), lambda i: (i, 0, 0))),
        compiler_params=_cparams(("parallel",)),
        name="rope_kmean",
    )(proj, proj, proj, cos, sin)


def _moba_head(i, q, ka_ref, vt_ref, km, qa_ref, a, nblk):
    blk = MOBA_BLOCK
    dh = ATT_HEAD_DIM
    nt = (((1,), (1,)), ((), ()))

    km_hi = km.astype(BF16)
    r1 = km - km_hi.astype(F32)
    km_mid = r1.astype(BF16)
    km_lo = (r1 - km_mid.astype(F32)).astype(BF16)
    gate = (lax.dot_general(km_hi, q, nt, preferred_element_type=F32)
            + lax.dot_general(km_mid, q, nt, preferred_element_type=F32)
            + lax.dot_general(km_lo, q, nt, preferred_element_type=F32))
    row = lax.broadcasted_iota(jnp.int32, (nblk, blk), 0).astype(F32)
    past = row < i.astype(F32)
    g = jnp.where(past, gate, NEG)
    bias = jnp.full((nblk, blk), NEG, F32)
    for _ in range(min(MOBA_TOPK, max(nblk - 1, 1))):
        mx = jnp.max(g, axis=0, keepdims=True)
        first = jnp.min(jnp.where(g == mx, row, float(nblk)), axis=0, keepdims=True)
        pick = row == first
        bias = jnp.where(pick, 0.0, bias)
        g = jnp.where(pick, -jnp.inf, g)
    bias = jnp.where(past, bias, NEG)
    bias_t = jnp.concatenate([bias, jnp.zeros((dh - nblk, blk), F32)], axis=0).T
    qa_ref[a, :, 0:dh] = q
    qa_ref[a, :, dh:2 * dh] = bias_t.astype(BF16)

    start = pl.multiple_of(i * blk, blk)
    k_own = ka_ref[pl.ds(start, blk), 2 * a * dh:(2 * a + 1) * dh]
    s = lax.dot_general(k_own, q, nt, preferred_element_type=F32)
    key = lax.broadcasted_iota(jnp.int32, (blk, blk), 0)
    qry = lax.broadcasted_iota(jnp.int32, (blk, blk), 1)
    s = jnp.where(key <= qry, s, NEG)
    m0 = jnp.max(s, axis=0, keepdims=True)
    p = jnp.exp2(s - m0)
    l0 = jnp.sum(p, axis=0, keepdims=True)
    acc0 = jnp.dot(vt_ref[i, a], p.astype(BF16), preferred_element_type=F32)
    return m0, l0, acc0


def _moba_kernel(q_ref, ka_ref, vt_ref, km_ref, o_ref, qa_ref, s_ref, *, nblk, hps, kt):
    i = pl.program_id(2)
    blk = MOBA_BLOCK
    dh = ATT_HEAD_DIM
    tk = kt * blk
    last_tile = nblk // kt - 1
    nt = (((1,), (1,)), ((), ()))
    s_slots = (s_ref.at[0], s_ref.at[1])
    init = []
    for a in range(hps):
        lanes = slice(a * dh, (a + 1) * dh)
        init.extend(_moba_head(i, q_ref[:, lanes], ka_ref, vt_ref, km_ref[:, lanes], qa_ref, a, nblk))

    def scores_into(slot, t):
        start = pl.multiple_of(jnp.minimum(t, last_tile) * tk, tk)
        for a in range(hps):
            s_slots[slot][a] = lax.dot_general(ka_ref[pl.ds(start, tk), 2 * a * dh:(2 * a + 2) * dh],
                                             qa_ref[a], nt, preferred_element_type=F32)

    def update(slot, t, state):
        for a in range(hps):
            m, l, acc = state[3 * a:3 * a + 3]
            s = s_slots[slot][a]
            m_new = jnp.maximum(m, jnp.max(s, axis=0, keepdims=True))
            alpha = jnp.exp2(m - m_new)
            p = jnp.exp2(s - m_new)
            l = alpha * l + jnp.sum(p, axis=0, keepdims=True)
            pb = p.astype(BF16)
            acc = alpha * acc
            for c in range(kt):
                acc = acc + jnp.dot(vt_ref[t * kt + c, a], pb[c * blk:(c + 1) * blk, :],
                                    preferred_element_type=F32)
            state[3 * a:3 * a + 3] = [m_new, l, acc]
        return state

    scores_into(0, 0)

    def body(jj, carry):
        state = list(carry)
        scores_into(1, 2 * jj + 1)
        state = update(0, 2 * jj, state)
        scores_into(0, 2 * jj + 2)
        state = update(1, jnp.minimum(2 * jj + 1, last_tile), state)
        return tuple(state)

    ntile = (i + kt - 1) // kt
    fin = lax.fori_loop(0, (ntile + 1) // 2, body, tuple(init))
    for a in range(hps):
        o_t = fin[3 * a + 2] / fin[3 * a + 1]
        o_ref[:, a * dh:(a + 1) * dh] = o_t.T.astype(o_ref.dtype)


MOBA_HEADS_PER_STEP = 2
MOBA_BLOCKS_PER_TILE = 2


def moba_attention(qr, ka, vt, kmean, batch, seq):
    nblk = seq // MOBA_BLOCK
    hps, kt = MOBA_HEADS_PER_STEP, MOBA_BLOCKS_PER_TILE
    assert ATT_HEADS % hps == 0 and nblk % kt == 0 and nblk <= ATT_HEAD_DIM
    w = hps * ATT_HEAD_DIM
    qspec = pl.BlockSpec((MOBA_BLOCK, w), lambda b, h, i: (b * nblk + i, h))
    return pl.pallas_call(
        functools.partial(_moba_kernel, nblk=nblk, hps=hps, kt=kt),
        out_shape=jax.ShapeDtypeStruct(qr.shape, BF16),
        grid=(batch, ATT_HEADS // hps, nblk),
        in_specs=[qspec,
                  pl.BlockSpec((seq, 2 * w), lambda b, h, i: (b, h)),
                  pl.BlockSpec((nblk, hps, ATT_HEAD_DIM, MOBA_BLOCK), lambda b, h, i: (b, h, 0, 0)),
                  pl.BlockSpec((None, nblk, w), lambda b, h, i: (b, 0, h))],
        out_specs=qspec,
        scratch_shapes=[pltpu.VMEM((hps, MOBA_BLOCK, 2 * ATT_HEAD_DIM), BF16),
                        pltpu.VMEM((2, hps, kt * MOBA_BLOCK, MOBA_BLOCK), F32)],
        compiler_params=_cparams(("parallel", "parallel", "arbitrary")),
        name="moba_attn",
    )(qr, ka, vt, kmean)


def _silu(x):
    return x * (0.5 * jnp.tanh(0.5 * x) + 0.5)


def _ssd_kernel(z0_ref, z1_ref, xs0_ref, xs1_ref, bc_ref, dt_ref, cwx_ref, cbx_ref, cwb_ref, cbb_ref,
                dtb_ref, alog_ref, dsk_ref, nw_ref, e3_ref, o_ref, xext_ref, bext_ref, state_ref,
                y_ref, *, heads):
    c = pl.program_id(1)
    q = SSD_CHUNK
    p = SSD_HEAD_DIM
    n = SSD_STATE
    hpg = heads // SSD_GROUPS
    gw = hpg * p

    @pl.when(c == 0)
    def _():
        xext_ref[0:HALO, :] = jnp.zeros((HALO, xext_ref.shape[1]), F32)
        bext_ref[0:HALO, :] = jnp.zeros((HALO, bext_ref.shape[1]), F32)
        state_ref[...] = jnp.zeros(state_ref.shape, F32)

    def conv_silu(ext_ref, raw, w_ref, b_ref):
        ext_ref[HALO:HALO + q, :] = raw
        acc = b_ref[...]
        for kk in range(SSD_CONV):
            off = HALO - (SSD_CONV - 1) + kk
            acc = acc + ext_ref[pl.ds(off, q), :] * w_ref[kk:kk + 1, :]
        ext_ref[0:HALO, :] = raw[q - HALO:, :]
        return _silu(acc)

    xs_raw = jnp.concatenate([xs0_ref[...], xs1_ref[...]], axis=1)
    xs = conv_silu(xext_ref, xs_raw, cwx_ref, cbx_ref)
    bc = conv_silu(bext_ref, bc_ref[...], cwb_ref, cbb_ref)

    dt = dt_ref[...] + dtb_ref[...]
    dt = jnp.maximum(dt, 0.0) + jnp.log1p(jnp.exp(-jnp.abs(dt)))
    a = -jnp.exp(alog_ref[...])
    da = dt * a
    tri = (lax.broadcasted_iota(jnp.int32, (q, q), 0)
           >= lax.broadcasted_iota(jnp.int32, (q, q), 1))
    acs = jnp.dot(tri.astype(F32), da, preferred_element_type=F32,
                  precision=lax.Precision.HIGHEST)
    acs2 = acs * LOG2E
    acs2_t = acs2.T
    hq = q // 2
    tri_h = tri[:hq, :hq]
    a_last = acs[q - 1:q, :]

    def per_channel(v):
        hi = v.astype(BF16)
        r1 = v - hi.astype(F32)
        mid = r1.astype(BF16)
        lo = (r1 - mid.astype(F32)).astype(BF16)
        return jnp.dot(jnp.concatenate([hi, mid, lo], axis=1), e3_ref[...],
                       preferred_element_type=F32)

    dec_in = per_channel(jnp.exp(acs))
    xdt = xs * per_channel(dt)
    xd = (xs * per_channel(dt * jnp.exp(a_last - acs))).astype(BF16)
    skip = xs * dsk_ref[...]
    first_head = lax.broadcasted_iota(jnp.int32, (q, 2 * p), 1) < p

    def decay_quadrants(cb, h):
        col, rw = acs2[:, h:h + 1], acs2_t[h:h + 1, :]
        m00 = cb[:hq, :hq] * jnp.where(tri_h, jnp.exp2(col[:hq] - rw[:, :hq]), 0.0)
        m10 = cb[hq:, :hq] * jnp.exp2(col[hq:] - rw[:, :hq])
        m11 = cb[hq:, hq:] * jnp.where(tri_h, jnp.exp2(col[hq:] - rw[:, hq:]), 0.0)
        return m00.astype(BF16), jnp.concatenate([m10, m11], axis=1).astype(BF16)

    for g in range(SSD_GROUPS):
        b_g = bc[:, g * n:(g + 1) * n]
        c_g = bc[:, (SSD_GROUPS + g) * n:(SSD_GROUPS + g + 1) * n]
        b_t = b_g.T.astype(BF16)
        c_b = c_g.astype(BF16)
        cb = jnp.dot(c_b, b_t, preferred_element_type=F32)
        gl = slice(g * gw, (g + 1) * gw)
        st_g = state_ref[g]
        y_off = jnp.dot(c_b, st_g.astype(BF16), preferred_element_type=F32)
        for j in range(0, hpg, 2):
            h = g * hpg + j
            pl_ = slice(h * p, (h + 2) * p)
            xp = xdt[:, pl_]
            xa = jnp.where(first_head, xp, 0.0).astype(BF16)
            xb = jnp.where(first_head, 0.0, xp).astype(BF16)
            top_a, bot_a = decay_quadrants(cb, h)
            top_b, bot_b = decay_quadrants(cb, h + 1)
            y_top = (jnp.dot(top_a, xa[:hq], preferred_element_type=F32)
                     + jnp.dot(top_b, xb[:hq], preferred_element_type=F32))
            y_bot = (jnp.dot(bot_a, xa, preferred_element_type=F32)
                     + jnp.dot(bot_b, xb, preferred_element_type=F32))
            y_ref[:, pl_] = (jnp.concatenate([y_top, y_bot], axis=0)
                             + y_off[:, j * p:(j + 2) * p] * dec_in[:, pl_] + skip[:, pl_])
        st_new = jnp.dot(b_t, xd[:, gl], preferred_element_type=F32)
        state_ref[g] = st_g * dec_in[q - 1:q, gl] + st_new

    zz = jnp.concatenate([z0_ref[...], z1_ref[...]], axis=1)
    y = y_ref[...] * _silu(zz)
    for g in range(SSD_GROUPS):
        yg = y[:, g * gw:(g + 1) * gw]
        yg = yg * lax.rsqrt(jnp.mean(yg * yg, axis=-1, keepdims=True) + EPS)
        o_ref[:, g * gw:(g + 1) * gw] = (yg * nw_ref[:, g * gw:(g + 1) * gw]).astype(o_ref.dtype)


def ssd_mixer(proj, dtp, z_col, xs_col, bc_col, dt_col, conv_w, conv_b, dt_bias, a_log, d_skip,
              norm_w, batch, seq, width):
    m = proj.shape[0]
    heads = width // SSD_HEAD_DIM
    nc = seq // SSD_CHUNK
    bcw = 2 * SSD_GROUPS * SSD_STATE
    assert heads <= LANES and (heads // SSD_GROUPS) % 2 == 0 and 2 * SSD_HEAD_DIM == LANES
    pad = lambda v: jnp.pad(v.astype(F32), (0, LANES - heads)).reshape(1, LANES)
    row = lambda w, col: pl.BlockSpec((SSD_CHUNK, w), lambda b, c: (b * nc + c, col))
    full = lambda r, w: pl.BlockSpec((r, w), lambda b, c: (0, 0))
    head_of = jnp.arange(width, dtype=jnp.int32) // SSD_HEAD_DIM
    e1 = (jnp.arange(LANES, dtype=jnp.int32)[:, None] == head_of[None, :]).astype(BF16)
    e3 = jnp.concatenate([e1, e1, e1], axis=0)
    return pl.pallas_call(
        functools.partial(_ssd_kernel, heads=heads),
        out_shape=jax.ShapeDtypeStruct((m, width), BF16),
        grid=(batch, nc),
        in_specs=[row(width // 2, z_col), row(width // 2, z_col + 1),
                  row(width // 2, xs_col), row(width // 2, xs_col + 1),
                  row(bcw, bc_col), row(LANES, dt_col),
                  full(SSD_CONV, width), full(1, width), full(SSD_CONV, bcw), full(1, bcw),
                  full(1, LANES), full(1, LANES), full(1, width), full(1, width),
                  full(3 * LANES, width)],
        out_specs=pl.BlockSpec((SSD_CHUNK, width), lambda b, c: (b * nc + c, 0)),
        scratch_shapes=[pltpu.VMEM((HALO + SSD_CHUNK, width), F32),
                        pltpu.VMEM((HALO + SSD_CHUNK, bcw), F32),
                        pltpu.VMEM((SSD_GROUPS, SSD_STATE, width // SSD_GROUPS), F32),
                        pltpu.VMEM((SSD_CHUNK, width), F32)],
        compiler_params=_cparams(("parallel", "arbitrary")),
        name="ssd",
    )(proj, proj, proj, proj, proj, dtp, conv_w[:, :width], conv_b[:width].reshape(1, width),
      conv_w[:, width:], conv_b[width:].reshape(1, bcw), pad(dt_bias), pad(a_log),
      jnp.repeat(d_skip.astype(F32), SSD_HEAD_DIM).reshape(1, width), norm_w.reshape(1, width), e3)


def _mem_attn_kernel(q_ref, k_ref, v_ref, o_ref):
    scale = MEM_HEAD_DIM ** -0.5
    nt = (((1,), (1,)), ((), ()))
    for h in range(MEM_HEADS):
        sl = slice(h * MEM_HEAD_DIM, (h + 1) * MEM_HEAD_DIM)
        q = q_ref[:, sl].astype(BF16)
        s = lax.dot_general(q, k_ref[:, sl], nt, preferred_element_type=F32) * scale
        p = jnp.exp(s - jnp.max(s, axis=1, keepdims=True))
        l = jnp.sum(p, axis=1, keepdims=True)
        o = jnp.dot(p.astype(BF16), v_ref[:, sl], preferred_element_type=F32)
        o_ref[:, sl] = (o / l).astype(o_ref.dtype)


def mem_attention(qp, kv, batch, seq, tq):
    w = MEM_HEADS * MEM_HEAD_DIM
    mem_len = kv.shape[0] // batch
    nq = seq // tq
    return pl.pallas_call(
        _mem_attn_kernel,
        out_shape=jax.ShapeDtypeStruct((batch * seq, w), BF16),
        grid=(batch, nq),
        in_specs=[pl.BlockSpec((tq, w), lambda b, i: (b * nq + i, 0)),
                  pl.BlockSpec((mem_len, w), lambda b, i: (b, 0)),
                  pl.BlockSpec((mem_len, w), lambda b, i: (b, 1))],
        out_specs=pl.BlockSpec((tq, w), lambda b, i: (b * nq + i, 0)),
        compiler_params=_cparams(("parallel", "parallel")),
        name="mem_attn",
    )(qp, kv, kv)


def _sigmoid(x):
    return 1.0 / (1.0 + jnp.exp(-x))


def _merge_kernel(ya_ref, ys_ref, ym_ref, ga_ref, gs_ref, gm_ref, wa_ref, ws_ref, wm_ref, o_ref):
    acc = _sigmoid(ga_ref[...]) * jnp.dot(ya_ref[...], wa_ref[...], preferred_element_type=F32)
    acc += _sigmoid(gs_ref[...]) * jnp.dot(ys_ref[...], ws_ref[...], preferred_element_type=F32)
    acc += _sigmoid(gm_ref[...]) * jnp.dot(ym_ref[...], wm_ref[...], preferred_element_type=F32)
    o_ref[...] = acc.astype(o_ref.dtype)


def branch_merge(ya, ys, ym, proj, gate_col, wa, ws, wm, tm, tn, layer=None):
    m = ya.shape[0]
    d = wa.shape[-1]
    per = d // tn
    g0 = gate_col // tn
    yspec = lambda w: pl.BlockSpec((tm, w), lambda i, j: (i, 0))
    gspec = lambda k: pl.BlockSpec((tm, tn), lambda i, j: (i, g0 + k * per + j))
    wspec = lambda w: _wspec(w, layer, (w.shape[-2], tn), lambda i, j: (0, j))
    return pl.pallas_call(
        _merge_kernel,
        out_shape=jax.ShapeDtypeStruct((m, d), BF16),
        grid=(m // tm, per),
        in_specs=[yspec(ya.shape[1]), yspec(ys.shape[1]), yspec(ym.shape[1]),
                  gspec(0), gspec(1), gspec(2),
                  wspec(wa), wspec(ws), wspec(wm)],
        out_specs=pl.BlockSpec((tm, tn), lambda i, j: (i, j)),
        compiler_params=_cparams(("parallel", "arbitrary")),
        name="branch_merge",
    )(ya, ys, ym, proj, proj, proj, wa, ws, wm)


def _out_residual_kernel(a_ref, w_ref, x_ref, g_ref, o_ref):
    y = jnp.dot(a_ref[...], w_ref[...], preferred_element_type=F32)
    o_ref[...] = x_ref[...] + _rms(y, g_ref[...])


def out_residual(a, w, x, gain, tm, layer=None):
    m, d = x.shape
    return pl.pallas_call(
        _out_residual_kernel,
        out_shape=jax.ShapeDtypeStruct((m, d), F32),
        grid=(m // tm,),
        in_specs=[pl.BlockSpec((tm, a.shape[1]), lambda i: (i, 0)),
                  _wspec(w, layer, w.shape[-2:], lambda i: (0, 0)),
                  pl.BlockSpec((tm, d), lambda i: (i, 0)),
                  pl.BlockSpec((1, d), lambda i: (0, 0))],
        out_specs=pl.BlockSpec((tm, d), lambda i: (i, 0)),
        input_output_aliases={2: 0},
        compiler_params=_cparams(("parallel",)),
        name="out_residual",
    )(a, w, x, gain.reshape(1, d))


def _gelu_tanh(x):
    return 0.5 * x * (1.0 + jnp.tanh(0.7978845608028654 * (x + 0.044715 * x * x * x)))


def _ffn_kernel(x_ref, xh_ref, gpre_ref, wa_ref, wg_ref, cwa_ref, cwg_ref, cba_ref, cbg_ref,
                wd_ref, gpost_ref, o_ref, hn_ref, u_ref, acc_ref, *, tm, seq, nf):
    i = pl.program_id(0)
    j = pl.program_id(1)
    seq_start = (i * tm) % seq == 0

    def up(a_ref, g_ref):
        for k, w_ref in enumerate((a_ref, g_ref)):
            u = jnp.dot(hn_ref[...], w_ref[...], preferred_element_type=F32)
            u_ref[k, 0:HALO, :] = jnp.where(seq_start, 0.0, u[0:HALO, :])
            u_ref[k, HALO:HALO + tm, :] = u[HALO:, :]

    def conv(k, cw_ref, cb_ref):
        out = cb_ref[...]
        for kk in range(FFN_CONV):
            off = HALO - (FFN_CONV - 1) + kk
            out = out + u_ref[k, pl.ds(off, tm), :] * cw_ref[kk:kk + 1, :]
        return out

    def consume():
        a = conv(0, cwa_ref, cba_ref)
        g = conv(1, cwg_ref, cbg_ref)
        act = (_gelu_tanh(a) * g).astype(BF16)
        acc_ref[...] += jnp.dot(act, wd_ref[...], preferred_element_type=F32)

    @pl.when(j == 0)
    def _():
        hn_ref[0:HALO, :] = _rms(xh_ref[...], gpre_ref[...]).astype(BF16)
        hn_ref[HALO:HALO + tm, :] = _rms(x_ref[...], gpre_ref[...]).astype(BF16)
        acc_ref[...] = jnp.zeros(acc_ref.shape, F32)

    up(wa_ref, wg_ref)
    consume()

    @pl.when(j == nf - 1)
    def _():
        o_ref[...] = x_ref[...] + _rms(acc_ref[...], gpost_ref[...])


def conv_glu_ffn(x, gpre, w_up, conv_w, conv_b, w_down, gpost, seq, tm, tf, layer=None):
    m, d = x.shape
    dff = w_down.shape[-2]
    nf = dff // tf
    hb = tm // HALO
    vec = lambda: pl.BlockSpec((1, d), lambda i, j: (0, 0))
    return pl.pallas_call(
        functools.partial(_ffn_kernel, tm=tm, seq=seq, nf=nf),
        out_shape=jax.ShapeDtypeStruct((m, d), F32),
        grid=(m // tm, nf),
        in_specs=[pl.BlockSpec((tm, d), lambda i, j: (i, 0)),
                  pl.BlockSpec((HALO, d), lambda i, j: (jnp.maximum(i * hb - 1, 0), 0)),
                  vec(),
                  _wspec(w_up, layer, (d, tf), lambda i, j: (0, j)),
                  _wspec(w_up, layer, (d, tf), lambda i, j: (0, nf + j)),
                  pl.BlockSpec((FFN_CONV, tf), lambda i, j: (0, j)),
                  pl.BlockSpec((FFN_CONV, tf), lambda i, j: (0, nf + j)),
                  pl.BlockSpec((1, tf), lambda i, j: (0, j)),
                  pl.BlockSpec((1, tf), lambda i, j: (0, nf + j)),
                  _wspec(w_down, layer, (tf, d), lambda i, j: (j, 0)),
                  vec()],
        out_specs=pl.BlockSpec((tm, d), lambda i, j: (i, 0)),
        scratch_shapes=[pltpu.VMEM((HALO + tm, d), BF16),
                        pltpu.VMEM((2, HALO + tm, tf), F32),
                        pltpu.VMEM((tm, d), F32)],
        compiler_params=_cparams(("parallel", "arbitrary")),
        name="conv_glu_ffn",
    )(x, x, gpre.reshape(1, d), w_up, w_up, conv_w, conv_w, conv_b.reshape(1, -1),
      conv_b.reshape(1, -1), w_down, gpost.reshape(1, d))


def _rope_tables(seq):
    half = ATT_HEAD_DIM // 2
    inv = ROPE_THETA ** (-jnp.arange(half, dtype=F32) / half)
    ang = jnp.arange(seq, dtype=F32)[:, None] * inv[None, :]
    cos, sin = jnp.cos(ang), jnp.sin(ang)
    return jnp.concatenate([cos, cos], axis=1), jnp.concatenate([-sin, sin], axis=1)


def kernel(x, mem, norm_mix_pre, norm_mix_post, norm_ffn_pre, norm_ffn_post, norm_mem, w_in,
           conv_ssd_w, conv_ssd_b, dt_bias, a_log, d_skip, ssd_norm, w_mem_kv, w_br_attn, w_br_ssd,
           w_br_mem, w_out, w_up, conv_ffn_w, conv_ffn_b, w_down):
    batch, seq, d = x.shape
    depth = w_in.shape[0]
    mem_len = mem.shape[1]
    ssd_w = w_br_ssd.shape[1]
    heads = ssd_w // SSD_HEAD_DIM
    bcw = 2 * SSD_GROUPS * SSD_STATE
    mem_w = MEM_HEADS * MEM_HEAD_DIM
    assert seq % MOBA_BLOCK == 0 and seq % SSD_CHUNK == 0 and d == ssd_w

    o_q, o_z = 0, 3 * ATT_WIDTH
    o_xs = o_z + ssd_w
    o_bc = o_xs + ssd_w
    o_dt = o_bc + bcw
    o_qm = o_dt + heads
    o_g = o_qm + mem_w

    half = ssd_w // 2
    assert o_z % half == 0 and o_xs % half == 0 and o_bc % bcw == 0 and o_q % ATT_WIDTH == 0

    cos, sin = _rope_tables(seq)
    xf = x.reshape(batch * seq, d)
    memf = mem.reshape(batch * mem_len, d)

    w_main = w_in[:, :, :o_dt].astype(BF16)
    w_gate = w_in[:, :, o_g:].astype(BF16)
    w_b = jnp.concatenate([w_in[:, :, o_qm:o_g], w_in[:, :, o_dt:o_qm],
                           jnp.zeros((depth, d, LANES - heads), F32)], axis=2).astype(BF16)
    w_kv, w_ba, w_bs, w_bm, w_o, w_u, w_d = (
        t.astype(BF16) for t in (w_mem_kv, w_br_attn, w_br_ssd, w_br_mem, w_out, w_up, w_down))

    for l in range(depth):
        proj = norm_matmul(xf, norm_mix_pre[l], w_main, F32, tm=1024, tn=1024, layer=l)
        gates = norm_matmul(xf, norm_mix_pre[l], w_gate, F32, tm=1024, tn=1024, layer=l)
        projb = norm_matmul(xf, norm_mix_pre[l], w_b, F32, tm=1024, tn=w_b.shape[2], layer=l)

        qr, ka, vt, kmean = rope_kmean(proj, cos, sin, o_q // ATT_WIDTH, seq)
        y_a = moba_attention(qr, ka, vt, kmean.reshape(batch, seq // MOBA_BLOCK, ATT_WIDTH),
                             batch, seq)
        y_s = ssd_mixer(proj, projb, o_z // half, o_xs // half, o_bc // bcw, mem_w // LANES,
                        conv_ssd_w[l], conv_ssd_b[l], dt_bias[l], a_log[l], d_skip[l], ssd_norm[l],
                        batch, seq, ssd_w)
        kv = norm_matmul(memf, norm_mem[l], w_kv, BF16, tm=batch * mem_len, tn=512, layer=l)
        y_m = mem_attention(projb, kv, batch, seq, tq=512)
        merged = branch_merge(y_a, y_s, y_m, gates, 0, w_ba, w_bs, w_bm, tm=1024, tn=512, layer=l)
        xf = out_residual(merged, w_o, xf, norm_mix_post[l], tm=512, layer=l)
        xf = conv_glu_ffn(xf, norm_ffn_pre[l], w_u, conv_ffn_w[l], conv_ffn_b[l], w_d,
                          norm_ffn_post[l], seq, tm=512, tf=512, layer=l)
    return xf.reshape(batch, seq, d)
```

```python
import functools

import jax
import jax.numpy as jnp
from jax import lax
from jax.experimental import pallas as pl
from jax.experimental.pallas import tpu as pltpu

F32 = jnp.float32
BF16 = jnp.bfloat16

EPS = 1e-6
NEG = -1e30
ROPE_THETA = 10000.0

ATT_HEADS = 8
ATT_HEAD_DIM = 128
ATT_WIDTH = ATT_HEADS * ATT_HEAD_DIM
MOBA_BLOCK = 256
MOBA_TOPK = 3
LOG2E = 1.4426950408889634
Q_PRESCALE = ATT_HEAD_DIM ** -0.5 * LOG2E

SSD_HEAD_DIM = 64
SSD_GROUPS = 4
SSD_STATE = 128
SSD_CONV = 4
SSD_CHUNK = 256

MEM_HEADS = 4
MEM_HEAD_DIM = 256
FFN_CONV = 3

LANES = 128
HALO = 16
VMEM_LIMIT = 56 * 1024 * 1024


def _cparams(semantics, vmem=VMEM_LIMIT):
    return pltpu.CompilerParams(dimension_semantics=semantics, vmem_limit_bytes=vmem)


def _rms(x, gain):
    return x * lax.rsqrt(jnp.mean(x * x, axis=-1, keepdims=True) + EPS) * gain


def _norm_matmul_kernel(x_ref, g_ref, w_ref, o_ref, hn_ref):
    @pl.when(pl.program_id(1) == 0)
    def _():
        hn_ref[...] = _rms(x_ref[...], g_ref[...]).astype(BF16)

    o_ref[...] = jnp.dot(hn_ref[...], w_ref[...], preferred_element_type=F32).astype(o_ref.dtype)


def _wspec(w, layer, block, index):
    if w.ndim == 2:
        return pl.BlockSpec(block, index)
    return pl.BlockSpec((None,) + tuple(block), lambda *g: (layer,) + tuple(index(*g)))


def norm_matmul(x, gain, w, out_dtype, tm, tn, layer=None, n=None):
    m, d = x.shape
    n = w.shape[-1] if n is None else n
    return pl.pallas_call(
        _norm_matmul_kernel,
        out_shape=jax.ShapeDtypeStruct((m, n), out_dtype),
        grid=(m // tm, n // tn),
        in_specs=[pl.BlockSpec((tm, d), lambda i, j: (i, 0)),
                  pl.BlockSpec((1, d), lambda i, j: (0, 0)),
                  _wspec(w, layer, (d, tn), lambda i, j: (0, j))],
        out_specs=pl.BlockSpec((tm, tn), lambda i, j: (i, j)),
        scratch_shapes=[pltpu.VMEM((tm, d), BF16)],
        compiler_params=_cparams(("parallel", "arbitrary")),
        name="norm_matmul",
    )(x, gain.reshape(1, d), w)


def _route_bias(q, km, blk_idx, nblk):
    blk = MOBA_BLOCK
    nt = (((1,), (1,)), ((), ()))
    km_hi = km.astype(BF16)
    r1 = km - km_hi.astype(F32)
    km_mid = r1.astype(BF16)
    km_lo = (r1 - km_mid.astype(F32)).astype(BF16)
    gate = (lax.dot_general(km_hi, q, nt, preferred_element_type=F32)
            + lax.dot_general(km_mid, q, nt, preferred_element_type=F32)
            + lax.dot_general(km_lo, q, nt, preferred_element_type=F32))
    row = lax.broadcasted_iota(jnp.int32, (nblk, blk), 0).astype(F32)
    past = row < blk_idx.astype(F32)
    g = jnp.where(past, gate, NEG)
    bias = jnp.full((nblk, blk), NEG, F32)
    for _ in range(min(MOBA_TOPK, max(nblk - 1, 1))):
        mx = jnp.max(g, axis=0, keepdims=True)
        first = jnp.min(jnp.where(g == mx, row, float(nblk)), axis=0, keepdims=True)
        pick = row == first
        bias = jnp.where(pick, 0.0, bias)
        g = jnp.where(pick, -jnp.inf, g)
    bias = jnp.where(past, bias, NEG)
    return jnp.concatenate([bias, jnp.zeros((ATT_HEAD_DIM - nblk, blk), F32)], axis=0).T


def _rope_kernel(q_ref, k_ref, v_ref, cos_ref, sin_ref, qa_ref, ka_ref, vt_ref, km_ref, *, per_seq):
    cos = cos_ref[...]
    sin = sin_ref[...]
    half = ATT_HEAD_DIM // 2
    dh = ATT_HEAD_DIM
    blk_idx = pl.program_id(0) % per_seq

    @pl.when(pl.program_id(0) == 0)
    def _():
        km_ref[...] = jnp.zeros(km_ref.shape, F32)

    lane = lax.broadcasted_iota(jnp.int32, (MOBA_BLOCK, dh), 1)
    onehot = jnp.where(lane == blk_idx, 1.0, 0.0).astype(BF16)
    this_blk = lax.broadcasted_iota(jnp.int32, (per_seq, dh), 0) == blk_idx
    for h in range(ATT_HEADS):
        sl = slice(h * dh, (h + 1) * dh)
        q = q_ref[:, sl]
        k = k_ref[:, sl]
        qr = q * cos + pltpu.roll(q, half, 1) * sin
        kr = k * cos + pltpu.roll(k, half, 1) * sin
        qb = (qr * Q_PRESCALE).astype(BF16)
        km = km_ref[:, sl]
        qa_ref[:, 2 * h * dh:(2 * h + 1) * dh] = qb
        qa_ref[:, (2 * h + 1) * dh:(2 * h + 2) * dh] = _route_bias(qb, km, blk_idx, per_seq).astype(BF16)
        ka_ref[:, 2 * h * dh:(2 * h + 1) * dh] = kr.astype(BF16)
        ka_ref[:, (2 * h + 1) * dh:(2 * h + 2) * dh] = onehot
        km_ref[:, sl] = jnp.where(this_blk, jnp.mean(kr, axis=0, keepdims=True), km)
        vt_ref[0, h] = v_ref[:, sl].T.astype(BF16)


def rope_route(proj, cos, sin, q_col, seq):
    m = proj.shape[0]
    nblk = m // MOBA_BLOCK
    per_seq = seq // MOBA_BLOCK
    assert per_seq <= ATT_HEAD_DIM
    row = lambda c: pl.BlockSpec((MOBA_BLOCK, ATT_WIDTH), lambda i: (i, c))
    tab = pl.BlockSpec((MOBA_BLOCK, ATT_HEAD_DIM), lambda i: (i % per_seq, 0))
    aug = pl.BlockSpec((MOBA_BLOCK, 2 * ATT_WIDTH), lambda i: (i, 0))
    return pl.pallas_call(
        functools.partial(_rope_kernel, per_seq=per_seq),
        out_shape=(jax.ShapeDtypeStruct((m, 2 * ATT_WIDTH), BF16),
                   jax.ShapeDtypeStruct((m, 2 * ATT_WIDTH), BF16),
                   jax.ShapeDtypeStruct((nblk, ATT_HEADS, ATT_HEAD_DIM, MOBA_BLOCK), BF16)),
        grid=(nblk,),
        in_specs=[row(q_col), row(q_col + 1), row(q_col + 2), tab, tab],
        out_specs=(aug, aug,
                   pl.BlockSpec((1, ATT_HEADS, ATT_HEAD_DIM, MOBA_BLOCK), lambda i: (i, 0, 0, 0))),
        scratch_shapes=[pltpu.VMEM((per_seq, ATT_WIDTH), F32)],
        compiler_params=_cparams(("arbitrary",)),
        name="rope_route",
    )(proj, proj, proj, cos, sin)


def _own_scores(i, q, ka_ref, a):
    dh = ATT_HEAD_DIM
    start = pl.multiple_of(i * MOBA_BLOCK, MOBA_BLOCK)
    k_own = ka_ref[pl.ds(start, MOBA_BLOCK), 2 * a * dh:(2 * a + 1) * dh]
    return lax.dot_general(k_own, q, (((1,), (1,)), ((), ())), preferred_element_type=F32)


def _own_state(i, s, vt_ref, a):
    blk = MOBA_BLOCK
    key = lax.broadcasted_iota(jnp.int32, (blk, blk), 0)
    qry = lax.broadcasted_iota(jnp.int32, (blk, blk), 1)
    s = jnp.where(key <= qry, s, NEG)
    m0 = jnp.max(s, axis=0, keepdims=True)
    p = jnp.exp2(s - m0)
    l0 = jnp.sum(p, axis=0, keepdims=True)
    acc0 = jnp.dot(vt_ref[i, a], p.astype(BF16), preferred_element_type=F32)
    return m0, l0, acc0


def _moba_kernel(qa_ref, ka_ref, vt_ref, o_ref, s_ref, *, nblk, hps, kt):
    i = pl.program_id(2)
    blk = MOBA_BLOCK
    dh = ATT_HEAD_DIM
    tk = kt * blk
    last_tile = nblk // kt - 1
    nt = (((1,), (1,)), ((), ()))
    s_slots = (s_ref.at[0], s_ref.at[1])
    def scores_into(slot, t):
        start = pl.multiple_of(jnp.minimum(t, last_tile) * tk, tk)
        for a in range(hps):
            aug = slice(2 * a * dh, (2 * a + 2) * dh)
            s_slots[slot][a] = lax.dot_general(ka_ref[pl.ds(start, tk), aug], qa_ref[:, aug], nt,
                                               preferred_element_type=F32)

    def update(slot, t, state):
        probs = []
        for a in range(hps):
            m, l, acc = state[3 * a:3 * a + 3]
            s = s_slots[slot][a]
            m_new = jnp.maximum(m, jnp.max(s, axis=0, keepdims=True))
            alpha = jnp.exp2(m - m_new)
            p = jnp.exp2(s - m_new)
            l = alpha * l + jnp.sum(p, axis=0, keepdims=True)
            probs.append(p.astype(BF16))
            state[3 * a:3 * a + 3] = [m_new, l, alpha * acc]
        for a in range(hps):
            acc = state[3 * a + 2]
            for c in range(kt):
                acc = acc + jnp.dot(vt_ref[t * kt + c, a], probs[a][c * blk:(c + 1) * blk, :],
                                    preferred_element_type=F32)
            state[3 * a + 2] = acc
        return state

    own = [_own_scores(i, qa_ref[:, 2 * a * dh:(2 * a + 1) * dh], ka_ref, a) for a in range(hps)]
    scores_into(0, 0)
    init = []
    for a in range(hps):
        init.extend(_own_state(i, own[a], vt_ref, a))


    def body(jj, carry):
        state = list(carry)
        scores_into(1, 2 * jj + 1)
        state = update(0, 2 * jj, state)
        scores_into(0, 2 * jj + 2)
        state = update(1, jnp.minimum(2 * jj + 1, last_tile), state)
        return tuple(state)

    ntile = (i + kt - 1) // kt
    fin = lax.fori_loop(0, (ntile + 1) // 2, body, tuple(init))
    for a in range(hps):
        o_t = fin[3 * a + 2] / fin[3 * a + 1]
        o_ref[:, a * dh:(a + 1) * dh] = o_t.T.astype(o_ref.dtype)


MOBA_HEADS_PER_STEP = 2
MOBA_BLOCKS_PER_TILE = 2


def moba_attention(qa, ka, vt, batch, seq):
    nblk = seq // MOBA_BLOCK
    hps, kt = MOBA_HEADS_PER_STEP, MOBA_BLOCKS_PER_TILE
    assert ATT_HEADS % hps == 0 and nblk % kt == 0 and nblk <= ATT_HEAD_DIM
    w = hps * ATT_HEAD_DIM
    return pl.pallas_call(
        functools.partial(_moba_kernel, nblk=nblk, hps=hps, kt=kt),
        out_shape=jax.ShapeDtypeStruct((qa.shape[0], ATT_WIDTH), BF16),
        grid=(batch, ATT_HEADS // hps, nblk),
        in_specs=[pl.BlockSpec((MOBA_BLOCK, 2 * w), lambda b, h, i: (b * nblk + i, h)),
                  pl.BlockSpec((seq, 2 * w), lambda b, h, i: (b, h)),
                  pl.BlockSpec((nblk, hps, ATT_HEAD_DIM, MOBA_BLOCK), lambda b, h, i: (b, h, 0, 0))],
        out_specs=pl.BlockSpec((MOBA_BLOCK, w), lambda b, h, i: (b * nblk + i, h)),
        scratch_shapes=[pltpu.VMEM((2, hps, kt * MOBA_BLOCK, MOBA_BLOCK), F32)],
        compiler_params=_cparams(("parallel", "parallel", "arbitrary")),
        name="moba_attn",
    )(qa, ka, vt)


def _silu(x):
    return x * (0.5 * jnp.tanh(0.5 * x) + 0.5)


def _ssd_kernel(z0_ref, z1_ref, xs0_ref, xs1_ref, bc_ref, dt_ref, cwx_ref, cbx_ref, cwb_ref, cbb_ref,
                dtb_ref, alog_ref, dsk_ref, nw_ref, e3_ref, o_ref, xext_ref, bext_ref, state_ref,
                y_ref, *, heads):
    c = pl.program_id(1)
    q = SSD_CHUNK
    p = SSD_HEAD_DIM
    n = SSD_STATE
    hpg = heads // SSD_GROUPS
    gw = hpg * p

    @pl.when(c == 0)
    def _():
        xext_ref[0:HALO, :] = jnp.zeros((HALO, xext_ref.shape[1]), F32)
        bext_ref[0:HALO, :] = jnp.zeros((HALO, bext_ref.shape[1]), F32)
        state_ref[...] = jnp.zeros(state_ref.shape, F32)

    def conv_silu(ext_ref, raw, w_ref, b_ref):
        ext_ref[HALO:HALO + q, :] = raw
        acc = b_ref[...]
        for kk in range(SSD_CONV):
            off = HALO - (SSD_CONV - 1) + kk
            acc = acc + ext_ref[pl.ds(off, q), :] * w_ref[kk:kk + 1, :]
        ext_ref[0:HALO, :] = raw[q - HALO:, :]
        return _silu(acc)

    xs_raw = jnp.concatenate([xs0_ref[...], xs1_ref[...]], axis=1)
    xs = conv_silu(xext_ref, xs_raw, cwx_ref, cbx_ref)
    bc = conv_silu(bext_ref, bc_ref[...], cwb_ref, cbb_ref)

    dt = dt_ref[...] + dtb_ref[...]
    dt = jnp.maximum(dt, 0.0) + jnp.log1p(jnp.exp(-jnp.abs(dt)))
    a = -jnp.exp(alog_ref[...])
    da = dt * a
    tri = (lax.broadcasted_iota(jnp.int32, (q, q), 0)
           >= lax.broadcasted_iota(jnp.int32, (q, q), 1))
    acs = jnp.dot(tri.astype(F32), da, preferred_element_type=F32,
                  precision=lax.Precision.HIGHEST)
    acs2 = acs * LOG2E
    acs2_t = acs2.T
    hq = q // 2
    tri_h = tri[:hq, :hq]
    a_last = acs[q - 1:q, :]

    def per_channel(v):
        hi = v.astype(BF16)
        r1 = v - hi.astype(F32)
        mid = r1.astype(BF16)
        lo = (r1 - mid.astype(F32)).astype(BF16)
        return jnp.dot(jnp.concatenate([hi, mid, lo], axis=1), e3_ref[...],
                       preferred_element_type=F32)

    dec_in = per_channel(jnp.exp(acs))
    xdt = xs * per_channel(dt)
    xd = (xs * per_channel(dt * jnp.exp(a_last - acs))).astype(BF16)
    skip = xs * dsk_ref[...]
    first_head = lax.broadcasted_iota(jnp.int32, (q, 2 * p), 1) < p

    def decay_quadrants(cb, h):
        col, rw = acs2[:, h:h + 1], acs2_t[h:h + 1, :]
        m00 = cb[:hq, :hq] * jnp.where(tri_h, jnp.exp2(col[:hq] - rw[:, :hq]), 0.0)
        m10 = cb[hq:, :hq] * jnp.exp2(col[hq:] - rw[:, :hq])
        m11 = cb[hq:, hq:] * jnp.where(tri_h, jnp.exp2(col[hq:] - rw[:, hq:]), 0.0)
        return m00.astype(BF16), jnp.concatenate([m10, m11], axis=1).astype(BF16)

    for g in range(SSD_GROUPS):
        b_g = bc[:, g * n:(g + 1) * n]
        c_g = bc[:, (SSD_GROUPS + g) * n:(SSD_GROUPS + g + 1) * n]
        b_t = b_g.T.astype(BF16)
        c_b = c_g.astype(BF16)
        cb = jnp.dot(c_b, b_t, preferred_element_type=F32)
        gl = slice(g * gw, (g + 1) * gw)
        st_g = state_ref[g]
        y_off = jnp.dot(c_b, st_g.astype(BF16), preferred_element_type=F32)
        for j in range(0, hpg, 2):
            h = g * hpg + j
            pl_ = slice(h * p, (h + 2) * p)
            xp = xdt[:, pl_]
            xa = jnp.where(first_head, xp, 0.0).astype(BF16)
            xb = jnp.where(first_head, 0.0, xp).astype(BF16)
            top_a, bot_a = decay_quadrants(cb, h)
            top_b, bot_b = decay_quadrants(cb, h + 1)
            y_top = (jnp.dot(top_a, xa[:hq], preferred_element_type=F32)
                     + jnp.dot(top_b, xb[:hq], preferred_element_type=F32))
            y_bot = (jnp.dot(bot_a, xa, preferred_element_type=F32)
                     + jnp.dot(bot_b, xb, preferred_element_type=F32))
            y_ref[:, pl_] = (jnp.concatenate([y_top, y_bot], axis=0)
                             + y_off[:, j * p:(j + 2) * p] * dec_in[:, pl_] + skip[:, pl_])
        st_new = jnp.dot(b_t, xd[:, gl], preferred_element_type=F32)
        state_ref[g] = st_g * dec_in[q - 1:q, gl] + st_new

    zz = jnp.concatenate([z0_ref[...], z1_ref[...]], axis=1)
    y = y_ref[...] * _silu(zz)
    for g in range(SSD_GROUPS):
        yg = y[:, g * gw:(g + 1) * gw]
        yg = yg * lax.rsqrt(jnp.mean(yg * yg, axis=-1, keepdims=True) + EPS)
        o_ref[:, g * gw:(g + 1) * gw] = (yg * nw_ref[:, g * gw:(g + 1) * gw]).astype(o_ref.dtype)


def ssd_mixer(proj, dtp, z_col, xs_col, bc_col, dt_col, conv_w, conv_b, dt_bias, a_log, d_skip,
              norm_w, batch, seq, width):
    m = proj.shape[0]
    heads = width // SSD_HEAD_DIM
    nc = seq // SSD_CHUNK
    bcw = 2 * SSD_GROUPS * SSD_STATE
    assert heads <= LANES and (heads // SSD_GROUPS) % 2 == 0 and 2 * SSD_HEAD_DIM == LANES
    pad = lambda v: jnp.pad(v.astype(F32), (0, LANES - heads)).reshape(1, LANES)
    row = lambda w, col: pl.BlockSpec((SSD_CHUNK, w), lambda b, c: (b * nc + c, col))
    full = lambda r, w: pl.BlockSpec((r, w), lambda b, c: (0, 0))
    head_of = jnp.arange(width, dtype=jnp.int32) // SSD_HEAD_DIM
    e1 = (jnp.arange(LANES, dtype=jnp.int32)[:, None] == head_of[None, :]).astype(BF16)
    e3 = jnp.concatenate([e1, e1, e1], axis=0)
    return pl.pallas_call(
        functools.partial(_ssd_kernel, heads=heads),
        out_shape=jax.ShapeDtypeStruct((m, width), BF16),
        grid=(batch, nc),
        in_specs=[row(width // 2, z_col), row(width // 2, z_col + 1),
                  row(width // 2, xs_col), row(width // 2, xs_col + 1),
                  row(bcw, bc_col), row(LANES, dt_col),
                  full(SSD_CONV, width), full(1, width), full(SSD_CONV, bcw), full(1, bcw),
                  full(1, LANES), full(1, LANES), full(1, width), full(1, width),
                  full(3 * LANES, width)],
        out_specs=pl.BlockSpec((SSD_CHUNK, width), lambda b, c: (b * nc + c, 0)),
        scratch_shapes=[pltpu.VMEM((HALO + SSD_CHUNK, width), F32),
                        pltpu.VMEM((HALO + SSD_CHUNK, bcw), F32),
                        pltpu.VMEM((SSD_GROUPS, SSD_STATE, width // SSD_GROUPS), F32),
                        pltpu.VMEM((SSD_CHUNK, width), F32)],
        compiler_params=_cparams(("parallel", "arbitrary")),
        name="ssd",
    )(proj, proj, proj, proj, proj, dtp, conv_w[:, :width], conv_b[:width].reshape(1, width),
      conv_w[:, width:], conv_b[width:].reshape(1, bcw), pad(dt_bias), pad(a_log),
      jnp.repeat(d_skip.astype(F32), SSD_HEAD_DIM).reshape(1, width), norm_w.reshape(1, width), e3)


def _mem_attn_kernel(q_ref, k_ref, v_ref, o_ref):
    scale = MEM_HEAD_DIM ** -0.5
    nt = (((1,), (1,)), ((), ()))
    for h in range(MEM_HEADS):
        sl = slice(h * MEM_HEAD_DIM, (h + 1) * MEM_HEAD_DIM)
        q = q_ref[:, sl].astype(BF16)
        s = lax.dot_general(q, k_ref[:, sl], nt, preferred_element_type=F32) * scale
        p = jnp.exp(s - jnp.max(s, axis=1, keepdims=True))
        l = jnp.sum(p, axis=1, keepdims=True)
        o = jnp.dot(p.astype(BF16), v_ref[:, sl], preferred_element_type=F32)
        o_ref[:, sl] = (o / l).astype(o_ref.dtype)


def mem_attention(qp, kv, batch, seq, tq):
    w = MEM_HEADS * MEM_HEAD_DIM
    mem_len = kv.shape[0] // batch
    nq = seq // tq
    return pl.pallas_call(
        _mem_attn_kernel,
        out_shape=jax.ShapeDtypeStruct((batch * seq, w), BF16),
        grid=(batch, nq),
        in_specs=[pl.BlockSpec((tq, w), lambda b, i: (b * nq + i, 0)),
                  pl.BlockSpec((mem_len, w), lambda b, i: (b, 0)),
                  pl.BlockSpec((mem_len, w), lambda b, i: (b, 1))],
        out_specs=pl.BlockSpec((tq, w), lambda b, i: (b * nq + i, 0)),
        compiler_params=_cparams(("parallel", "parallel")),
        name="mem_attn",
    )(qp, kv, kv)


def _sigmoid(x):
    return 1.0 / (1.0 + jnp.exp(-x))


def _merge_kernel(ya_ref, ys_ref, ym_ref, ga_ref, gs_ref, gm_ref, wa_ref, ws_ref, wm_ref, o_ref):
    acc = _sigmoid(ga_ref[...]) * jnp.dot(ya_ref[...], wa_ref[...], preferred_element_type=F32)
    acc += _sigmoid(gs_ref[...]) * jnp.dot(ys_ref[...], ws_ref[...], preferred_element_type=F32)
    acc += _sigmoid(gm_ref[...]) * jnp.dot(ym_ref[...], wm_ref[...], preferred_element_type=F32)
    o_ref[...] = acc.astype(o_ref.dtype)


def branch_merge(ya, ys, ym, proj, gate_col, wa, ws, wm, tm, tn, layer=None):
    m = ya.shape[0]
    d = wa.shape[-1]
    per = d // tn
    g0 = gate_col // tn
    yspec = lambda w: pl.BlockSpec((tm, w), lambda i, j: (i, 0))
    gspec = lambda k: pl.BlockSpec((tm, tn), lambda i, j: (i, g0 + k * per + j))
    wspec = lambda w: _wspec(w, layer, (w.shape[-2], tn), lambda i, j: (0, j))
    return pl.pallas_call(
        _merge_kernel,
        out_shape=jax.ShapeDtypeStruct((m, d), BF16),
        grid=(m // tm, per),
        in_specs=[yspec(ya.shape[1]), yspec(ys.shape[1]), yspec(ym.shape[1]),
                  gspec(0), gspec(1), gspec(2),
                  wspec(wa), wspec(ws), wspec(wm)],
        out_specs=pl.BlockSpec((tm, tn), lambda i, j: (i, j)),
        compiler_params=_cparams(("parallel", "arbitrary")),
        name="branch_merge",
    )(ya, ys, ym, proj, proj, proj, wa, ws, wm)


def _out_residual_kernel(a_ref, w_ref, x_ref, g_ref, o_ref):
    y = jnp.dot(a_ref[...], w_ref[...], preferred_element_type=F32)
    o_ref[...] = x_ref[...] + _rms(y, g_ref[...])


def out_residual(a, w, x, gain, tm, layer=None):
    m, d = x.shape
    return pl.pallas_call(
        _out_residual_kernel,
        out_shape=jax.ShapeDtypeStruct((m, d), F32),
        grid=(m // tm,),
        in_specs=[pl.BlockSpec((tm, a.shape[1]), lambda i: (i, 0)),
                  _wspec(w, layer, w.shape[-2:], lambda i: (0, 0)),
                  pl.BlockSpec((tm, d), lambda i: (i, 0)),
                  pl.BlockSpec((1, d), lambda i: (0, 0))],
        out_specs=pl.BlockSpec((tm, d), lambda i: (i, 0)),
        input_output_aliases={2: 0},
        compiler_params=_cparams(("parallel",)),
        name="out_residual",
    )(a, w, x, gain.reshape(1, d))


def _gelu_tanh(x):
    return 0.5 * x * (1.0 + jnp.tanh(0.7978845608028654 * (x + 0.044715 * x * x * x)))


def _ffn_kernel(x_ref, xh_ref, gpre_ref, wa_ref, wg_ref, cwa_ref, cwg_ref, cba_ref, cbg_ref,
                wd_ref, gpost_ref, o_ref, hn_ref, u_ref, acc_ref, *, tm, seq, nf):
    i = pl.program_id(0)
    j = pl.program_id(1)
    seq_start = (i * tm) % seq == 0

    def up(a_ref, g_ref):
        for k, w_ref in enumerate((a_ref, g_ref)):
            u = jnp.dot(hn_ref[...], w_ref[...], preferred_element_type=F32)
            u_ref[k, 0:HALO, :] = jnp.where(seq_start, 0.0, u[0:HALO, :])
            u_ref[k, HALO:HALO + tm, :] = u[HALO:, :]

    def conv(k, cw_ref, cb_ref):
        out = cb_ref[...]
        for kk in range(FFN_CONV):
            off = HALO - (FFN_CONV - 1) + kk
            out = out + u_ref[k, pl.ds(off, tm), :] * cw_ref[kk:kk + 1, :]
        return out

    def consume():
        a = conv(0, cwa_ref, cba_ref)
        g = conv(1, cwg_ref, cbg_ref)
        act = (_gelu_tanh(a) * g).astype(BF16)
        acc_ref[...] += jnp.dot(act, wd_ref[...], preferred_element_type=F32)

    @pl.when(j == 0)
    def _():
        hn_ref[0:HALO, :] = _rms(xh_ref[...], gpre_ref[...]).astype(BF16)
        hn_ref[HALO:HALO + tm, :] = _rms(x_ref[...], gpre_ref[...]).astype(BF16)
        acc_ref[...] = jnp.zeros(acc_ref.shape, F32)

    up(wa_ref, wg_ref)
    consume()

    @pl.when(j == nf - 1)
    def _():
        o_ref[...] = x_ref[...] + _rms(acc_ref[...], gpost_ref[...])


def conv_glu_ffn(x, gpre, w_up, conv_w, conv_b, w_down, gpost, seq, tm, tf, layer=None):
    m, d = x.shape
    dff = w_down.shape[-2]
    nf = dff // tf
    hb = tm // HALO
    vec = lambda: pl.BlockSpec((1, d), lambda i, j: (0, 0))
    return pl.pallas_call(
        functools.partial(_ffn_kernel, tm=tm, seq=seq, nf=nf),
        out_shape=jax.ShapeDtypeStruct((m, d), F32),
        grid=(m // tm, nf),
        in_specs=[pl.BlockSpec((tm, d), lambda i, j: (i, 0)),
                  pl.BlockSpec((HALO, d), lambda i, j: (jnp.maximum(i * hb - 1, 0), 0)),
                  vec(),
                  _wspec(w_up, layer, (d, tf), lambda i, j: (0, j)),
                  _wspec(w_up, layer, (d, tf), lambda i, j: (0, nf + j)),
                  pl.BlockSpec((FFN_CONV, tf), lambda i, j: (0, j)),
                  pl.BlockSpec((FFN_CONV, tf), lambda i, j: (0, nf + j)),
                  pl.BlockSpec((1, tf), lambda i, j: (0, j)),
                  pl.BlockSpec((1, tf), lambda i, j: (0, nf + j)),
                  _wspec(w_down, layer, (tf, d), lambda i, j: (j, 0)),
                  vec()],
        out_specs=pl.BlockSpec((tm, d), lambda i, j: (i, 0)),
        scratch_shapes=[pltpu.VMEM((HALO + tm, d), BF16),
                        pltpu.VMEM((2, HALO + tm, tf), F32),
                        pltpu.VMEM((tm, d), F32)],
        compiler_params=_cparams(("parallel", "arbitrary")),
        name="conv_glu_ffn",
    )(x, x, gpre.reshape(1, d), w_up, w_up, conv_w, conv_w, conv_b.reshape(1, -1),
      conv_b.reshape(1, -1), w_down, gpost.reshape(1, d))


def _rope_tables(seq):
    half = ATT_HEAD_DIM // 2
    inv = ROPE_THETA ** (-jnp.arange(half, dtype=F32) / half)
    ang = jnp.arange(seq, dtype=F32)[:, None] * inv[None, :]
    cos, sin = jnp.cos(ang), jnp.sin(ang)
    return jnp.concatenate([cos, cos], axis=1), jnp.concatenate([-sin, sin], axis=1)


def kernel(x, mem, norm_mix_pre, norm_mix_post, norm_ffn_pre, norm_ffn_post, norm_mem, w_in,
           conv_ssd_w, conv_ssd_b, dt_bias, a_log, d_skip, ssd_norm, w_mem_kv, w_br_attn, w_br_ssd,
           w_br_mem, w_out, w_up, conv_ffn_w, conv_ffn_b, w_down):
    batch, seq, d = x.shape
    depth = w_in.shape[0]
    mem_len = mem.shape[1]
    ssd_w = w_br_ssd.shape[1]
    heads = ssd_w // SSD_HEAD_DIM
    bcw = 2 * SSD_GROUPS * SSD_STATE
    mem_w = MEM_HEADS * MEM_HEAD_DIM
    assert seq % MOBA_BLOCK == 0 and seq % SSD_CHUNK == 0 and d == ssd_w

    o_q, o_z = 0, 3 * ATT_WIDTH
    o_xs = o_z + ssd_w
    o_bc = o_xs + ssd_w
    o_dt = o_bc + bcw
    o_qm = o_dt + heads
    o_g = o_qm + mem_w

    half = ssd_w // 2
    assert o_z % half == 0 and o_xs % half == 0 and o_bc % bcw == 0 and o_q % ATT_WIDTH == 0

    cos, sin = _rope_tables(seq)
    xf = x.reshape(batch * seq, d)
    memf = mem.reshape(batch * mem_len, d)

    w_all = w_in.astype(BF16)
    w_gate = w_all[:, :, o_g:]
    w_b = jnp.concatenate([w_all[:, :, o_qm:o_g], w_all[:, :, o_dt:o_qm],
                           jnp.zeros((depth, d, LANES - heads), BF16)], axis=2)
    w_kv, w_ba, w_bs, w_bm, w_o, w_u, w_d = (
        t.astype(BF16) for t in (w_mem_kv, w_br_attn, w_br_ssd, w_br_mem, w_out, w_up, w_down))

    for l in range(depth):
        proj = norm_matmul(xf, norm_mix_pre[l], w_all, F32, tm=1024, tn=1024, layer=l, n=o_dt)
        gates = norm_matmul(xf, norm_mix_pre[l], w_gate, F32, tm=1024, tn=1024, layer=l)
        projb = norm_matmul(xf, norm_mix_pre[l], w_b, F32, tm=1024, tn=w_b.shape[2], layer=l)

        qa, ka, vt = rope_route(proj, cos, sin, o_q // ATT_WIDTH, seq)
        y_a = moba_attention(qa, ka, vt, batch, seq)
        y_s = ssd_mixer(proj, projb, o_z // half, o_xs // half, o_bc // bcw, mem_w // LANES,
                        conv_ssd_w[l], conv_ssd_b[l], dt_bias[l], a_log[l], d_skip[l], ssd_norm[l],
                        batch, seq, ssd_w)
        kv = norm_matmul(memf, norm_mem[l], w_kv, BF16, tm=batch * mem_len, tn=512, layer=l)
        y_m = mem_attention(projb, kv, batch, seq, tq=512)
        merged = branch_merge(y_a, y_s, y_m, gates, 0, w_ba, w_bs, w_bm, tm=1024, tn=512, layer=l)
        xf = out_residual(merged, w_o, xf, norm_mix_post[l], tm=512, layer=l)
        xf = conv_glu_ffn(xf, norm_ffn_pre[l], w_u, conv_ffn_w[l], conv_ffn_b[l], w_d,
                          norm_ffn_post[l], seq, tm=512, tf=512, layer=l)
    return xf.reshape(batch, seq, d)
```

```python
import functools

import jax
import jax.numpy as jnp
from jax import lax
from jax.experimental import pallas as pl
from jax.experimental.pallas import tpu as pltpu

F32 = jnp.float32
BF16 = jnp.bfloat16

EPS = 1e-6
NEG = -1e30
ROPE_THETA = 10000.0

ATT_HEADS = 8
ATT_HEAD_DIM = 128
ATT_WIDTH = ATT_HEADS * ATT_HEAD_DIM
MOBA_BLOCK = 256
MOBA_TOPK = 3
LOG2E = 1.4426950408889634
Q_PRESCALE = ATT_HEAD_DIM ** -0.5 * LOG2E

SSD_HEAD_DIM = 64
SSD_GROUPS = 4
SSD_STATE = 128
SSD_CONV = 4
SSD_CHUNK = 256

MEM_HEADS = 4
MEM_HEAD_DIM = 256
FFN_CONV = 3

LANES = 128
HALO = 16
VMEM_LIMIT = 56 * 1024 * 1024


def _cparams(semantics, vmem=VMEM_LIMIT):
    return pltpu.CompilerParams(dimension_semantics=semantics, vmem_limit_bytes=vmem)


def _rms(x, gain):
    return x * lax.rsqrt(jnp.mean(x * x, axis=-1, keepdims=True) + EPS) * gain


def _norm_matmul_kernel(x_ref, g_ref, w_ref, o_ref, hn_ref):
    @pl.when(pl.program_id(1) == 0)
    def _():
        hn_ref[...] = _rms(x_ref[...], g_ref[...]).astype(BF16)

    o_ref[...] = jnp.dot(hn_ref[...], w_ref[...], preferred_element_type=F32).astype(o_ref.dtype)


def _wspec(w, layer, block, index):
    if w.ndim == 2:
        return pl.BlockSpec(block, index)
    return pl.BlockSpec((None,) + tuple(block), lambda *g: (layer,) + tuple(index(*g)))


def norm_matmul(x, gain, w, out_dtype, tm, tn, layer=None):
    m, d = x.shape
    n = w.shape[-1]
    return pl.pallas_call(
        _norm_matmul_kernel,
        out_shape=jax.ShapeDtypeStruct((m, n), out_dtype),
        grid=(m // tm, n // tn),
        in_specs=[pl.BlockSpec((tm, d), lambda i, j: (i, 0)),
                  pl.BlockSpec((1, d), lambda i, j: (0, 0)),
                  _wspec(w, layer, (d, tn), lambda i, j: (0, j))],
        out_specs=pl.BlockSpec((tm, tn), lambda i, j: (i, j)),
        scratch_shapes=[pltpu.VMEM((tm, d), BF16)],
        compiler_params=_cparams(("parallel", "arbitrary")),
        name="norm_matmul",
    )(x, gain.reshape(1, d), w)


def _route_bias(q, km, blk_idx, nblk):
    blk = MOBA_BLOCK
    nt = (((1,), (1,)), ((), ()))
    km_hi = km.astype(BF16)
    r1 = km - km_hi.astype(F32)
    km_mid = r1.astype(BF16)
    km_lo = (r1 - km_mid.astype(F32)).astype(BF16)
    gate = (lax.dot_general(km_hi, q, nt, preferred_element_type=F32)
            + lax.dot_general(km_mid, q, nt, preferred_element_type=F32)
            + lax.dot_general(km_lo, q, nt, preferred_element_type=F32))
    row = lax.broadcasted_iota(jnp.int32, (nblk, blk), 0).astype(F32)
    past = row < blk_idx.astype(F32)
    g = jnp.where(past, gate, NEG)
    bias = jnp.full((nblk, blk), NEG, F32)
    for _ in range(min(MOBA_TOPK, max(nblk - 1, 1))):
        mx = jnp.max(g, axis=0, keepdims=True)
        first = jnp.min(jnp.where(g == mx, row, float(nblk)), axis=0, keepdims=True)
        pick = row == first
        bias = jnp.where(pick, 0.0, bias)
        g = jnp.where(pick, -jnp.inf, g)
    bias = jnp.where(past, bias, NEG)
    return jnp.concatenate([bias, jnp.zeros((ATT_HEAD_DIM - nblk, blk), F32)], axis=0).T


def _rope_kernel(q_ref, k_ref, v_ref, cos_ref, sin_ref, qa_ref, ka_ref, vt_ref, km_ref, *, per_seq):
    cos = cos_ref[...]
    sin = sin_ref[...]
    half = ATT_HEAD_DIM // 2
    dh = ATT_HEAD_DIM
    blk_idx = pl.program_id(0) % per_seq

    @pl.when(pl.program_id(0) == 0)
    def _():
        km_ref[...] = jnp.zeros(km_ref.shape, F32)

    lane = lax.broadcasted_iota(jnp.int32, (MOBA_BLOCK, dh), 1)
    onehot = jnp.where(lane == blk_idx, 1.0, 0.0).astype(BF16)
    this_blk = lax.broadcasted_iota(jnp.int32, (per_seq, dh), 0) == blk_idx
    for h in range(ATT_HEADS):
        sl = slice(h * dh, (h + 1) * dh)
        q = q_ref[:, sl]
        k = k_ref[:, sl]
        qr = q * cos + pltpu.roll(q, half, 1) * sin
        kr = k * cos + pltpu.roll(k, half, 1) * sin
        qb = (qr * Q_PRESCALE).astype(BF16)
        km = km_ref[:, sl]
        qa_ref[:, 2 * h * dh:(2 * h + 1) * dh] = qb
        qa_ref[:, (2 * h + 1) * dh:(2 * h + 2) * dh] = _route_bias(qb, km, blk_idx, per_seq).astype(BF16)
        ka_ref[:, 2 * h * dh:(2 * h + 1) * dh] = kr.astype(BF16)
        ka_ref[:, (2 * h + 1) * dh:(2 * h + 2) * dh] = onehot
        km_ref[:, sl] = jnp.where(this_blk, jnp.mean(kr, axis=0, keepdims=True), km)
        vt_ref[0, h] = v_ref[:, sl].T.astype(BF16)


def rope_route(proj, cos, sin, q_col, seq):
    m = proj.shape[0]
    nblk = m // MOBA_BLOCK
    per_seq = seq // MOBA_BLOCK
    assert per_seq <= ATT_HEAD_DIM
    row = lambda c: pl.BlockSpec((MOBA_BLOCK, ATT_WIDTH), lambda i: (i, c))
    tab = pl.BlockSpec((MOBA_BLOCK, ATT_HEAD_DIM), lambda i: (i % per_seq, 0))
    aug = pl.BlockSpec((MOBA_BLOCK, 2 * ATT_WIDTH), lambda i: (i, 0))
    return pl.pallas_call(
        functools.partial(_rope_kernel, per_seq=per_seq),
        out_shape=(jax.ShapeDtypeStruct((m, 2 * ATT_WIDTH), BF16),
                   jax.ShapeDtypeStruct((m, 2 * ATT_WIDTH), BF16),
                   jax.ShapeDtypeStruct((nblk, ATT_HEADS, ATT_HEAD_DIM, MOBA_BLOCK), BF16)),
        grid=(nblk,),
        in_specs=[row(q_col), row(q_col + 1), row(q_col + 2), tab, tab],
        out_specs=(aug, aug,
                   pl.BlockSpec((1, ATT_HEADS, ATT_HEAD_DIM, MOBA_BLOCK), lambda i: (i, 0, 0, 0))),
        scratch_shapes=[pltpu.VMEM((per_seq, ATT_WIDTH), F32)],
        compiler_params=_cparams(("arbitrary",)),
        name="rope_route",
    )(proj, proj, proj, cos, sin)


def _own_scores(i, q, ka_ref, a):
    dh = ATT_HEAD_DIM
    start = pl.multiple_of(i * MOBA_BLOCK, MOBA_BLOCK)
    k_own = ka_ref[pl.ds(start, MOBA_BLOCK), 2 * a * dh:(2 * a + 1) * dh]
    return lax.dot_general(k_own, q, (((1,), (1,)), ((), ())), preferred_element_type=F32)


def _own_state(i, s, vt_ref, a):
    blk = MOBA_BLOCK
    key = lax.broadcasted_iota(jnp.int32, (blk, blk), 0)
    qry = lax.broadcasted_iota(jnp.int32, (blk, blk), 1)
    s = jnp.where(key <= qry, s, NEG)
    m0 = jnp.max(s, axis=0, keepdims=True)
    p = jnp.exp2(s - m0)
    l0 = jnp.sum(p, axis=0, keepdims=True)
    acc0 = jnp.dot(vt_ref[i, a], p.astype(BF16), preferred_element_type=F32)
    return m0, l0, acc0


def _moba_kernel(qa_ref, ka_ref, vt_ref, o_ref, s_ref, *, nblk, hps, kt):
    i = pl.program_id(2)
    blk = MOBA_BLOCK
    dh = ATT_HEAD_DIM
    tk = kt * blk
    last_tile = nblk // kt - 1
    nt = (((1,), (1,)), ((), ()))
    s_slots = (s_ref.at[0], s_ref.at[1])
    def scores_into(slot, t):
        start = pl.multiple_of(jnp.minimum(t, last_tile) * tk, tk)
        for a in range(hps):
            aug = slice(2 * a * dh, (2 * a + 2) * dh)
            s_slots[slot][a] = lax.dot_general(ka_ref[pl.ds(start, tk), aug], qa_ref[:, aug], nt,
                                               preferred_element_type=F32)

    def update(slot, t, state):
        probs = []
        for a in range(hps):
            m, l, acc = state[3 * a:3 * a + 3]
            s = s_slots[slot][a]
            m_new = jnp.maximum(m, jnp.max(s, axis=0, keepdims=True))
            alpha = jnp.exp2(m - m_new)
            p = jnp.exp2(s - m_new)
            l = alpha * l + jnp.sum(p, axis=0, keepdims=True)
            probs.append(p.astype(BF16))
            state[3 * a:3 * a + 3] = [m_new, l, alpha * acc]
        for a in range(hps):
            acc = state[3 * a + 2]
            for c in range(kt):
                acc = acc + jnp.dot(vt_ref[t * kt + c, a], probs[a][c * blk:(c + 1) * blk, :],
                                    preferred_element_type=F32)
            state[3 * a + 2] = acc
        return state

    own = [_own_scores(i, qa_ref[:, 2 * a * dh:(2 * a + 1) * dh], ka_ref, a) for a in range(hps)]
    scores_into(0, 0)
    init = []
    for a in range(hps):
        init.extend(_own_state(i, own[a], vt_ref, a))


    def body(jj, carry):
        state = list(carry)
        scores_into(1, 2 * jj + 1)
        state = update(0, 2 * jj, state)
        scores_into(0, 2 * jj + 2)
        state = update(1, jnp.minimum(2 * jj + 1, last_tile), state)
        return tuple(state)

    ntile = (i + kt - 1) // kt
    fin = lax.fori_loop(0, (ntile + 1) // 2, body, tuple(init))
    for a in range(hps):
        o_t = fin[3 * a + 2] / fin[3 * a + 1]
        o_ref[:, a * dh:(a + 1) * dh] = o_t.T.astype(o_ref.dtype)


MOBA_HEADS_PER_STEP = 2
MOBA_BLOCKS_PER_TILE = 2


def moba_attention(qa, ka, vt, batch, seq):
    nblk = seq // MOBA_BLOCK
    hps, kt = MOBA_HEADS_PER_STEP, MOBA_BLOCKS_PER_TILE
    assert ATT_HEADS % hps == 0 and nblk % kt == 0 and nblk <= ATT_HEAD_DIM
    w = hps * ATT_HEAD_DIM
    return pl.pallas_call(
        functools.partial(_moba_kernel, nblk=nblk, hps=hps, kt=kt),
        out_shape=jax.ShapeDtypeStruct((qa.shape[0], ATT_WIDTH), BF16),
        grid=(batch, ATT_HEADS // hps, nblk),
        in_specs=[pl.BlockSpec((MOBA_BLOCK, 2 * w), lambda b, h, i: (b * nblk + i, h)),
                  pl.BlockSpec((seq, 2 * w), lambda b, h, i: (b, h)),
                  pl.BlockSpec((nblk, hps, ATT_HEAD_DIM, MOBA_BLOCK), lambda b, h, i: (b, h, 0, 0))],
        out_specs=pl.BlockSpec((MOBA_BLOCK, w), lambda b, h, i: (b * nblk + i, h)),
        scratch_shapes=[pltpu.VMEM((2, hps, kt * MOBA_BLOCK, MOBA_BLOCK), F32)],
        compiler_params=_cparams(("parallel", "parallel", "arbitrary")),
        name="moba_attn",
    )(qa, ka, vt)


def _silu(x):
    return x * (0.5 * jnp.tanh(0.5 * x) + 0.5)


def _ssd_kernel(z0_ref, z1_ref, xs0_ref, xs1_ref, bc_ref, dt_ref, cwx_ref, cbx_ref, cwb_ref, cbb_ref,
                dtb_ref, alog_ref, dsk_ref, nw_ref, e3_ref, o_ref, xext_ref, bext_ref, state_ref,
                y_ref, *, heads):
    c = pl.program_id(1)
    q = SSD_CHUNK
    p = SSD_HEAD_DIM
    n = SSD_STATE
    hpg = heads // SSD_GROUPS
    gw = hpg * p

    @pl.when(c == 0)
    def _():
        xext_ref[0:HALO, :] = jnp.zeros((HALO, xext_ref.shape[1]), F32)
        bext_ref[0:HALO, :] = jnp.zeros((HALO, bext_ref.shape[1]), F32)
        state_ref[...] = jnp.zeros(state_ref.shape, F32)

    def conv_silu(ext_ref, raw, w_ref, b_ref):
        ext_ref[HALO:HALO + q, :] = raw
        acc = b_ref[...]
        for kk in range(SSD_CONV):
            off = HALO - (SSD_CONV - 1) + kk
            acc = acc + ext_ref[pl.ds(off, q), :] * w_ref[kk:kk + 1, :]
        ext_ref[0:HALO, :] = raw[q - HALO:, :]
        return _silu(acc)

    xs_raw = jnp.concatenate([xs0_ref[...], xs1_ref[...]], axis=1)
    xs = conv_silu(xext_ref, xs_raw, cwx_ref, cbx_ref)
    bc = conv_silu(bext_ref, bc_ref[...], cwb_ref, cbb_ref)

    dt = dt_ref[...] + dtb_ref[...]
    dt = jnp.maximum(dt, 0.0) + jnp.log1p(jnp.exp(-jnp.abs(dt)))
    a = -jnp.exp(alog_ref[...])
    da = dt * a
    tri = (lax.broadcasted_iota(jnp.int32, (q, q), 0)
           >= lax.broadcasted_iota(jnp.int32, (q, q), 1))
    acs = jnp.dot(tri.astype(F32), da, preferred_element_type=F32,
                  precision=lax.Precision.HIGHEST)
    acs2 = acs * LOG2E
    acs2_t = acs2.T
    hq = q // 2
    tri_h = tri[:hq, :hq]
    a_last = acs[q - 1:q, :]

    def per_channel(v):
        hi = v.astype(BF16)
        r1 = v - hi.astype(F32)
        mid = r1.astype(BF16)
        lo = (r1 - mid.astype(F32)).astype(BF16)
        return jnp.dot(jnp.concatenate([hi, mid, lo], axis=1), e3_ref[...],
                       preferred_element_type=F32)

    dec_in = per_channel(jnp.exp(acs))
    xdt = xs * per_channel(dt)
    xd = (xs * per_channel(dt * jnp.exp(a_last - acs))).astype(BF16)
    skip = xs * dsk_ref[...]
    first_head = lax.broadcasted_iota(jnp.int32, (q, 2 * p), 1) < p

    def decay_quadrants(cb, h):
        col, rw = acs2[:, h:h + 1], acs2_t[h:h + 1, :]
        m00 = cb[:hq, :hq] * jnp.where(tri_h, jnp.exp2(col[:hq] - rw[:, :hq]), 0.0)
        m10 = cb[hq:, :hq] * jnp.exp2(col[hq:] - rw[:, :hq])
        m11 = cb[hq:, hq:] * jnp.where(tri_h, jnp.exp2(col[hq:] - rw[:, hq:]), 0.0)
        return m00.astype(BF16), jnp.concatenate([m10, m11], axis=1).astype(BF16)

    for g in range(SSD_GROUPS):
        b_g = bc[:, g * n:(g + 1) * n]
        c_g = bc[:, (SSD_GROUPS + g) * n:(SSD_GROUPS + g + 1) * n]
        b_t = b_g.T.astype(BF16)
        c_b = c_g.astype(BF16)
        cb = jnp.dot(c_b, b_t, preferred_element_type=F32)
        gl = slice(g * gw, (g + 1) * gw)
        st_g = state_ref[g]
        y_off = jnp.dot(c_b, st_g.astype(BF16), preferred_element_type=F32)
        for j in range(0, hpg, 2):
            h = g * hpg + j
            pl_ = slice(h * p, (h + 2) * p)
            xp = xdt[:, pl_]
            xa = jnp.where(first_head, xp, 0.0).astype(BF16)
            xb = jnp.where(first_head, 0.0, xp).astype(BF16)
            top_a, bot_a = decay_quadrants(cb, h)
            top_b, bot_b = decay_quadrants(cb, h + 1)
            y_top = (jnp.dot(top_a, xa[:hq], preferred_element_type=F32)
                     + jnp.dot(top_b, xb[:hq], preferred_element_type=F32))
            y_bot = (jnp.dot(bot_a, xa, preferred_element_type=F32)
                     + jnp.dot(bot_b, xb, preferred_element_type=F32))
            y_ref[:, pl_] = (jnp.concatenate([y_top, y_bot], axis=0)
                             + y_off[:, j * p:(j + 2) * p] * dec_in[:, pl_] + skip[:, pl_])
        st_new = jnp.dot(b_t, xd[:, gl], preferred_element_type=F32)
        state_ref[g] = st_g * dec_in[q - 1:q, gl] + st_new

    zz = jnp.concatenate([z0_ref[...], z1_ref[...]], axis=1)
    y = y_ref[...] * _silu(zz)
    for g in range(SSD_GROUPS):
        yg = y[:, g * gw:(g + 1) * gw]
        yg = yg * lax.rsqrt(jnp.mean(yg * yg, axis=-1, keepdims=True) + EPS)
        o_ref[:, g * gw:(g + 1) * gw] = (yg * nw_ref[:, g * gw:(g + 1) * gw]).astype(o_ref.dtype)


def ssd_mixer(proj, dtp, z_col, xs_col, bc_col, dt_col, conv_w, conv_b, dt_bias, a_log, d_skip,
              norm_w, batch, seq, width):
    m = proj.shape[0]
    heads = width // SSD_HEAD_DIM
    nc = seq // SSD_CHUNK
    bcw = 2 * SSD_GROUPS * SSD_STATE
    assert heads <= LANES and (heads // SSD_GROUPS) % 2 == 0 and 2 * SSD_HEAD_DIM == LANES
    pad = lambda v: jnp.pad(v.astype(F32), (0, LANES - heads)).reshape(1, LANES)
    row = lambda w, col: pl.BlockSpec((SSD_CHUNK, w), lambda b, c: (b * nc + c, col))
    full = lambda r, w: pl.BlockSpec((r, w), lambda b, c: (0, 0))
    head_of = jnp.arange(width, dtype=jnp.int32) // SSD_HEAD_DIM
    e1 = (jnp.arange(LANES, dtype=jnp.int32)[:, None] == head_of[None, :]).astype(BF16)
    e3 = jnp.concatenate([e1, e1, e1], axis=0)
    return pl.pallas_call(
        functools.partial(_ssd_kernel, heads=heads),
        out_shape=jax.ShapeDtypeStruct((m, width), BF16),
        grid=(batch, nc),
        in_specs=[row(width // 2, z_col), row(width // 2, z_col + 1),
                  row(width // 2, xs_col), row(width // 2, xs_col + 1),
                  row(bcw, bc_col), row(LANES, dt_col),
                  full(SSD_CONV, width), full(1, width), full(SSD_CONV, bcw), full(1, bcw),
                  full(1, LANES), full(1, LANES), full(1, width), full(1, width),
                  full(3 * LANES, width)],
        out_specs=pl.BlockSpec((SSD_CHUNK, width), lambda b, c: (b * nc + c, 0)),
        scratch_shapes=[pltpu.VMEM((HALO + SSD_CHUNK, width), F32),
                        pltpu.VMEM((HALO + SSD_CHUNK, bcw), F32),
                        pltpu.VMEM((SSD_GROUPS, SSD_STATE, width // SSD_GROUPS), F32),
                        pltpu.VMEM((SSD_CHUNK, width), F32)],
        compiler_params=_cparams(("parallel", "arbitrary")),
        name="ssd",
    )(proj, proj, proj, proj, proj, dtp, conv_w[:, :width], conv_b[:width].reshape(1, width),
      conv_w[:, width:], conv_b[width:].reshape(1, bcw), pad(dt_bias), pad(a_log),
      jnp.repeat(d_skip.astype(F32), SSD_HEAD_DIM).reshape(1, width), norm_w.reshape(1, width), e3)


def _mem_attn_kernel(q_ref, k_ref, v_ref, o_ref):
    scale = MEM_HEAD_DIM ** -0.5
    nt = (((1,), (1,)), ((), ()))
    for h in range(MEM_HEADS):
        sl = slice(h * MEM_HEAD_DIM, (h + 1) * MEM_HEAD_DIM)
        q = q_ref[:, sl].astype(BF16)
        s = lax.dot_general(q, k_ref[:, sl], nt, preferred_element_type=F32) * scale
        p = jnp.exp(s - jnp.max(s, axis=1, keepdims=True))
        l = jnp.sum(p, axis=1, keepdims=True)
        o = jnp.dot(p.astype(BF16), v_ref[:, sl], preferred_element_type=F32)
        o_ref[:, sl] = (o / l).astype(o_ref.dtype)


def mem_attention(qp, kv, batch, seq, tq):
    w = MEM_HEADS * MEM_HEAD_DIM
    mem_len = kv.shape[0] // batch
    nq = seq // tq
    return pl.pallas_call(
        _mem_attn_kernel,
        out_shape=jax.ShapeDtypeStruct((batch * seq, w), BF16),
        grid=(batch, nq),
        in_specs=[pl.BlockSpec((tq, w), lambda b, i: (b * nq + i, 0)),
                  pl.BlockSpec((mem_len, w), lambda b, i: (b, 0)),
                  pl.BlockSpec((mem_len, w), lambda b, i: (b, 1))],
        out_specs=pl.BlockSpec((tq, w), lambda b, i: (b * nq + i, 0)),
        compiler_params=_cparams(("parallel", "parallel")),
        name="mem_attn",
    )(qp, kv, kv)


def _sigmoid(x):
    return 1.0 / (1.0 + jnp.exp(-x))


def _merge_kernel(ya_ref, ys_ref, ym_ref, ga_ref, gs_ref, gm_ref, wa_ref, ws_ref, wm_ref, o_ref):
    acc = _sigmoid(ga_ref[...]) * jnp.dot(ya_ref[...], wa_ref[...], preferred_element_type=F32)
    acc += _sigmoid(gs_ref[...]) * jnp.dot(ys_ref[...], ws_ref[...], preferred_element_type=F32)
    acc += _sigmoid(gm_ref[...]) * jnp.dot(ym_ref[...], wm_ref[...], preferred_element_type=F32)
    o_ref[...] = acc.astype(o_ref.dtype)


def branch_merge(ya, ys, ym, proj, gate_col, wa, ws, wm, tm, tn, layer=None):
    m = ya.shape[0]
    d = wa.shape[-1]
    per = d // tn
    g0 = gate_col // tn
    yspec = lambda w: pl.BlockSpec((tm, w), lambda i, j: (i, 0))
    gspec = lambda k: pl.BlockSpec((tm, tn), lambda i, j: (i, g0 + k * per + j))
    wspec = lambda w: _wspec(w, layer, (w.shape[-2], tn), lambda i, j: (0, j))
    return pl.pallas_call(
        _merge_kernel,
        out_shape=jax.ShapeDtypeStruct((m, d), BF16),
        grid=(m // tm, per),
        in_specs=[yspec(ya.shape[1]), yspec(ys.shape[1]), yspec(ym.shape[1]),
                  gspec(0), gspec(1), gspec(2),
                  wspec(wa), wspec(ws), wspec(wm)],
        out_specs=pl.BlockSpec((tm, tn), lambda i, j: (i, j)),
        compiler_params=_cparams(("parallel", "arbitrary")),
        name="branch_merge",
    )(ya, ys, ym, proj, proj, proj, wa, ws, wm)


def _out_residual_kernel(a_ref, w_ref, x_ref, g_ref, o_ref):
    y = jnp.dot(a_ref[...], w_ref[...], preferred_element_type=F32)
    o_ref[...] = x_ref[...] + _rms(y, g_ref[...])


def out_residual(a, w, x, gain, tm, layer=None):
    m, d = x.shape
    return pl.pallas_call(
        _out_residual_kernel,
        out_shape=jax.ShapeDtypeStruct((m, d), F32),
        grid=(m // tm,),
        in_specs=[pl.BlockSpec((tm, a.shape[1]), lambda i: (i, 0)),
                  _wspec(w, layer, w.shape[-2:], lambda i: (0, 0)),
                  pl.BlockSpec((tm, d), lambda i: (i, 0)),
                  pl.BlockSpec((1, d), lambda i: (0, 0))],
        out_specs=pl.BlockSpec((tm, d), lambda i: (i, 0)),
        input_output_aliases={2: 0},
        compiler_params=_cparams(("parallel",)),
        name="out_residual",
    )(a, w, x, gain.reshape(1, d))


def _gelu_tanh(x):
    return 0.5 * x * (1.0 + jnp.tanh(0.7978845608028654 * (x + 0.044715 * x * x * x)))


def _ffn_kernel(x_ref, xh_ref, gpre_ref, wa_ref, wg_ref, cwa_ref, cwg_ref, cba_ref, cbg_ref,
                wd_ref, gpost_ref, o_ref, hn_ref, u_ref, acc_ref, *, tm, seq, nf):
    i = pl.program_id(0)
    j = pl.program_id(1)
    seq_start = (i * tm) % seq == 0

    def up(a_ref, g_ref):
        for k, w_ref in enumerate((a_ref, g_ref)):
            u = jnp.dot(hn_ref[...], w_ref[...], preferred_element_type=F32)
            u_ref[k, 0:HALO, :] = jnp.where(seq_start, 0.0, u[0:HALO, :])
            u_ref[k, HALO:HALO + tm, :] = u[HALO:, :]

    def conv(k, cw_ref, cb_ref):
        out = cb_ref[...]
        for kk in range(FFN_CONV):
            off = HALO - (FFN_CONV - 1) + kk
            out = out + u_ref[k, pl.ds(off, tm), :] * cw_ref[kk:kk + 1, :]
        return out

    def consume():
        a = conv(0, cwa_ref, cba_ref)
        g = conv(1, cwg_ref, cbg_ref)
        act = (_gelu_tanh(a) * g).astype(BF16)
        acc_ref[...] += jnp.dot(act, wd_ref[...], preferred_element_type=F32)

    @pl.when(j == 0)
    def _():
        hn_ref[0:HALO, :] = _rms(xh_ref[...], gpre_ref[...]).astype(BF16)
        hn_ref[HALO:HALO + tm, :] = _rms(x_ref[...], gpre_ref[...]).astype(BF16)
        acc_ref[...] = jnp.zeros(acc_ref.shape, F32)

    up(wa_ref, wg_ref)
    consume()

    @pl.when(j == nf - 1)
    def _():
        o_ref[...] = x_ref[...] + _rms(acc_ref[...], gpost_ref[...])


def conv_glu_ffn(x, gpre, w_up, conv_w, conv_b, w_down, gpost, seq, tm, tf, layer=None):
    m, d = x.shape
    dff = w_down.shape[-2]
    nf = dff // tf
    hb = tm // HALO
    vec = lambda: pl.BlockSpec((1, d), lambda i, j: (0, 0))
    return pl.pallas_call(
        functools.partial(_ffn_kernel, tm=tm, seq=seq, nf=nf),
        out_shape=jax.ShapeDtypeStruct((m, d), F32),
        grid=(m // tm, nf),
        in_specs=[pl.BlockSpec((tm, d), lambda i, j: (i, 0)),
                  pl.BlockSpec((HALO, d), lambda i, j: (jnp.maximum(i * hb - 1, 0), 0)),
                  vec(),
                  _wspec(w_up, layer, (d, tf), lambda i, j: (0, j)),
                  _wspec(w_up, layer, (d, tf), lambda i, j: (0, nf + j)),
                  pl.BlockSpec((FFN_CONV, tf), lambda i, j: (0, j)),
                  pl.BlockSpec((FFN_CONV, tf), lambda i, j: (0, nf + j)),
                  pl.BlockSpec((1, tf), lambda i, j: (0, j)),
                  pl.BlockSpec((1, tf), lambda i, j: (0, nf + j)),
                  _wspec(w_down, layer, (tf, d), lambda i, j: (j, 0)),
                  vec()],
        out_specs=pl.BlockSpec((tm, d), lambda i, j: (i, 0)),
        scratch_shapes=[pltpu.VMEM((HALO + tm, d), BF16),
                        pltpu.VMEM((2, HALO + tm, tf), F32),
                        pltpu.VMEM((tm, d), F32)],
        compiler_params=_cparams(("parallel", "arbitrary")),
        name="conv_glu_ffn",
    )(x, x, gpre.reshape(1, d), w_up, w_up, conv_w, conv_w, conv_b.reshape(1, -1),
      conv_b.reshape(1, -1), w_down, gpost.reshape(1, d))


def _repack_kernel(w_ref, main_ref, gate_ref, b_ref, *, o_dt, o_qm, o_g):
    w = w_ref[...]
    rows = w.shape[0]
    pad = jnp.zeros((rows, b_ref.shape[1] - (o_g - o_dt)), F32)
    main_ref[...] = w[:, :o_dt].astype(BF16)
    gate_ref[...] = w[:, o_g:].astype(BF16)
    b_ref[...] = jnp.concatenate([w[:, o_qm:o_g], w[:, o_dt:o_qm], pad], axis=1).astype(BF16)


def repack_w_in(w_in, o_dt, o_qm, o_g, rows=128):
    depth, d, n_in = w_in.shape
    nb = -(-(o_g - o_dt) // LANES) * LANES
    spec = lambda n: pl.BlockSpec((None, rows, n), lambda l, i: (l, i, 0))
    return pl.pallas_call(
        functools.partial(_repack_kernel, o_dt=o_dt, o_qm=o_qm, o_g=o_g),
        out_shape=(jax.ShapeDtypeStruct((depth, d, o_dt), BF16),
                   jax.ShapeDtypeStruct((depth, d, n_in - o_g), BF16),
                   jax.ShapeDtypeStruct((depth, d, nb), BF16)),
        grid=(depth, d // rows),
        in_specs=[spec(n_in)],
        out_specs=(spec(o_dt), spec(n_in - o_g), spec(nb)),
        compiler_params=_cparams(("parallel", "parallel")),
        name="repack_w_in",
    )(w_in)


def _rope_tables(seq):
    half = ATT_HEAD_DIM // 2
    inv = ROPE_THETA ** (-jnp.arange(half, dtype=F32) / half)
    ang = jnp.arange(seq, dtype=F32)[:, None] * inv[None, :]
    cos, sin = jnp.cos(ang), jnp.sin(ang)
    return jnp.concatenate([cos, cos], axis=1), jnp.concatenate([-sin, sin], axis=1)


def kernel(x, mem, norm_mix_pre, norm_mix_post, norm_ffn_pre, norm_ffn_post, norm_mem, w_in,
           conv_ssd_w, conv_ssd_b, dt_bias, a_log, d_skip, ssd_norm, w_mem_kv, w_br_attn, w_br_ssd,
           w_br_mem, w_out, w_up, conv_ffn_w, conv_ffn_b, w_down):
    batch, seq, d = x.shape
    depth = w_in.shape[0]
    mem_len = mem.shape[1]
    ssd_w = w_br_ssd.shape[1]
    heads = ssd_w // SSD_HEAD_DIM
    bcw = 2 * SSD_GROUPS * SSD_STATE
    mem_w = MEM_HEADS * MEM_HEAD_DIM
    assert seq % MOBA_BLOCK == 0 and seq % SSD_CHUNK == 0 and d == ssd_w

    o_q, o_z = 0, 3 * ATT_WIDTH
    o_xs = o_z + ssd_w
    o_bc = o_xs + ssd_w
    o_dt = o_bc + bcw
    o_qm = o_dt + heads
    o_g = o_qm + mem_w

    half = ssd_w // 2
    assert o_z % half == 0 and o_xs % half == 0 and o_bc % bcw == 0 and o_q % ATT_WIDTH == 0

    cos, sin = _rope_tables(seq)
    xf = x.reshape(batch * seq, d)
    memf = mem.reshape(batch * mem_len, d)

    w_main, w_gate, w_b = repack_w_in(w_in, o_dt, o_qm, o_g)
    w_kv, w_ba, w_bs, w_bm, w_o, w_u, w_d = (
        t.astype(BF16) for t in (w_mem_kv, w_br_attn, w_br_ssd, w_br_mem, w_out, w_up, w_down))

    for l in range(depth):
        proj = norm_matmul(xf, norm_mix_pre[l], w_main, F32, tm=1024, tn=1024, layer=l)
        gates = norm_matmul(xf, norm_mix_pre[l], w_gate, F32, tm=1024, tn=1024, layer=l)
        projb = norm_matmul(xf, norm_mix_pre[l], w_b, F32, tm=1024, tn=w_b.shape[2], layer=l)

        qa, ka, vt = rope_route(proj, cos, sin, o_q // ATT_WIDTH, seq)
        y_a = moba_attention(qa, ka, vt, batch, seq)
        y_s = ssd_mixer(proj, projb, o_z // half, o_xs // half, o_bc // bcw, mem_w // LANES,
                        conv_ssd_w[l], conv_ssd_b[l], dt_bias[l], a_log[l], d_skip[l], ssd_norm[l],
                        batch, seq, ssd_w)
        kv = norm_matmul(memf, norm_mem[l], w_kv, BF16, tm=batch * mem_len, tn=512, layer=l)
        y_m = mem_attention(projb, kv, batch, seq, tq=512)
        merged = branch_merge(y_a, y_s, y_m, gates, 0, w_ba, w_bs, w_bm, tm=1024, tn=512, layer=l)
        xf = out_residual(merged, w_o, xf, norm_mix_post[l], tm=512, layer=l)
        xf = conv_glu_ffn(xf, norm_ffn_pre[l], w_u, conv_ffn_w[l], conv_ffn_b[l], w_d,
                          norm_ffn_post[l], seq, tm=512, tf=512, layer=l)
    return xf.reshape(batch, seq, d)
```

```python
import functools

import jax
import jax.numpy as jnp
from jax import lax
from jax.experimental import pallas as pl
from jax.experimental.pallas import tpu as pltpu

F32 = jnp.float32
BF16 = jnp.bfloat16

EPS = 1e-6
NEG = -1e30
ROPE_THETA = 10000.0

ATT_HEADS = 8
ATT_HEAD_DIM = 128
ATT_WIDTH = ATT_HEADS * ATT_HEAD_DIM
MOBA_BLOCK = 256
MOBA_TOPK = 3
LOG2E = 1.4426950408889634
Q_PRESCALE = ATT_HEAD_DIM ** -0.5 * LOG2E

SSD_HEAD_DIM = 64
SSD_GROUPS = 4
SSD_STATE = 128
SSD_CONV = 4
SSD_CHUNK = 256

MEM_HEADS = 4
MEM_HEAD_DIM = 256
FFN_CONV = 3

LANES = 128
HALO = 16
VMEM_LIMIT = 56 * 1024 * 1024


def _cparams(semantics, vmem=VMEM_LIMIT):
    return pltpu.CompilerParams(dimension_semantics=semantics, vmem_limit_bytes=vmem)


def _rms(x, gain):
    return x * lax.rsqrt(jnp.mean(x * x, axis=-1, keepdims=True) + EPS) * gain


def _norm_matmul_kernel(x_ref, g_ref, w_ref, o_ref, hn_ref):
    @pl.when(pl.program_id(1) == 0)
    def _():
        hn_ref[...] = _rms(x_ref[...], g_ref[...]).astype(BF16)

    o_ref[...] = jnp.dot(hn_ref[...], w_ref[...], preferred_element_type=F32).astype(o_ref.dtype)


def _wspec(w, layer, block, index):
    if w.ndim == 2:
        return pl.BlockSpec(block, index)
    return pl.BlockSpec((None,) + tuple(block), lambda *g: (layer,) + tuple(index(*g)))


def norm_matmul(x, gain, w, out_dtype, tm, tn, layer=None, n=None):
    m, d = x.shape
    n = w.shape[-1] if n is None else n
    return pl.pallas_call(
        _norm_matmul_kernel,
        out_shape=jax.ShapeDtypeStruct((m, n), out_dtype),
        grid=(m // tm, n // tn),
        in_specs=[pl.BlockSpec((tm, d), lambda i, j: (i, 0)),
                  pl.BlockSpec((1, d), lambda i, j: (0, 0)),
                  _wspec(w, layer, (d, tn), lambda i, j: (0, j))],
        out_specs=pl.BlockSpec((tm, tn), lambda i, j: (i, j)),
        scratch_shapes=[pltpu.VMEM((tm, d), BF16)],
        compiler_params=_cparams(("parallel", "arbitrary")),
        name="norm_matmul",
    )(x, gain.reshape(1, d), w)


def _route_bias(q, km, blk_idx, nblk):
    blk = MOBA_BLOCK
    nt = (((1,), (1,)), ((), ()))
    km_hi = km.astype(BF16)
    r1 = km - km_hi.astype(F32)
    km_mid = r1.astype(BF16)
    km_lo = (r1 - km_mid.astype(F32)).astype(BF16)
    gate = (lax.dot_general(km_hi, q, nt, preferred_element_type=F32)
            + lax.dot_general(km_mid, q, nt, preferred_element_type=F32)
            + lax.dot_general(km_lo, q, nt, preferred_element_type=F32))
    row = lax.broadcasted_iota(jnp.int32, (nblk, blk), 0).astype(F32)
    past = row < blk_idx.astype(F32)
    g = jnp.where(past, gate, NEG)
    bias = jnp.full((nblk, blk), NEG, F32)
    for _ in range(min(MOBA_TOPK, max(nblk - 1, 1))):
        mx = jnp.max(g, axis=0, keepdims=True)
        first = jnp.min(jnp.where(g == mx, row, float(nblk)), axis=0, keepdims=True)
        pick = row == first
        bias = jnp.where(pick, 0.0, bias)
        g = jnp.where(pick, -jnp.inf, g)
    bias = jnp.where(past, bias, NEG)
    return jnp.concatenate([bias, jnp.zeros((ATT_HEAD_DIM - nblk, blk), F32)], axis=0).T


def _rope_kernel(q_ref, k_ref, v_ref, cos_ref, sin_ref, qa_ref, ka_ref, vt_ref, km_ref, *, per_seq):
    cos = cos_ref[...]
    sin = sin_ref[...]
    half = ATT_HEAD_DIM // 2
    dh = ATT_HEAD_DIM
    blk_idx = pl.program_id(0) % per_seq

    @pl.when(pl.program_id(0) == 0)
    def _():
        km_ref[...] = jnp.zeros(km_ref.shape, F32)

    lane = lax.broadcasted_iota(jnp.int32, (MOBA_BLOCK, dh), 1)
    onehot = jnp.where(lane == blk_idx, 1.0, 0.0).astype(BF16)
    this_blk = lax.broadcasted_iota(jnp.int32, (per_seq, dh), 0) == blk_idx
    for h in range(ATT_HEADS):
        sl = slice(h * dh, (h + 1) * dh)
        q = q_ref[:, sl]
        k = k_ref[:, sl]
        qr = q * cos + pltpu.roll(q, half, 1) * sin
        kr = k * cos + pltpu.roll(k, half, 1) * sin
        qb = (qr * Q_PRESCALE).astype(BF16)
        km = km_ref[:, sl]
        qa_ref[:, 2 * h * dh:(2 * h + 1) * dh] = qb
        qa_ref[:, (2 * h + 1) * dh:(2 * h + 2) * dh] = _route_bias(qb, km, blk_idx, per_seq).astype(BF16)
        ka_ref[:, 2 * h * dh:(2 * h + 1) * dh] = kr.astype(BF16)
        ka_ref[:, (2 * h + 1) * dh:(2 * h + 2) * dh] = onehot
        km_ref[:, sl] = jnp.where(this_blk, jnp.mean(kr, axis=0, keepdims=True), km)
        vt_ref[0, h] = v_ref[:, sl].T.astype(BF16)


def rope_route(proj, cos, sin, q_col, seq):
    m = proj.shape[0]
    nblk = m // MOBA_BLOCK
    per_seq = seq // MOBA_BLOCK
    assert per_seq <= ATT_HEAD_DIM
    row = lambda c: pl.BlockSpec((MOBA_BLOCK, ATT_WIDTH), lambda i: (i, c))
    tab = pl.BlockSpec((MOBA_BLOCK, ATT_HEAD_DIM), lambda i: (i % per_seq, 0))
    aug = pl.BlockSpec((MOBA_BLOCK, 2 * ATT_WIDTH), lambda i: (i, 0))
    return pl.pallas_call(
        functools.partial(_rope_kernel, per_seq=per_seq),
        out_shape=(jax.ShapeDtypeStruct((m, 2 * ATT_WIDTH), BF16),
                   jax.ShapeDtypeStruct((m, 2 * ATT_WIDTH), BF16),
                   jax.ShapeDtypeStruct((nblk, ATT_HEADS, ATT_HEAD_DIM, MOBA_BLOCK), BF16)),
        grid=(nblk,),
        in_specs=[row(q_col), row(q_col + 1), row(q_col + 2), tab, tab],
        out_specs=(aug, aug,
                   pl.BlockSpec((1, ATT_HEADS, ATT_HEAD_DIM, MOBA_BLOCK), lambda i: (i, 0, 0, 0))),
        scratch_shapes=[pltpu.VMEM((per_seq, ATT_WIDTH), F32)],
        compiler_params=_cparams(("arbitrary",)),
        name="rope_route",
    )(proj, proj, proj, cos, sin)


def _own_scores(i, q, ka_ref, a):
    dh = ATT_HEAD_DIM
    start = pl.multiple_of(i * MOBA_BLOCK, MOBA_BLOCK)
    k_own = ka_ref[pl.ds(start, MOBA_BLOCK), 2 * a * dh:(2 * a + 1) * dh]
    return lax.dot_general(k_own, q, (((1,), (1,)), ((), ())), preferred_element_type=F32)


def _own_state(i, s, vt_ref, a):
    blk = MOBA_BLOCK
    key = lax.broadcasted_iota(jnp.int32, (blk, blk), 0)
    qry = lax.broadcasted_iota(jnp.int32, (blk, blk), 1)
    s = jnp.where(key <= qry, s, NEG)
    m0 = jnp.max(s, axis=0, keepdims=True)
    p = jnp.exp2(s - m0)
    l0 = jnp.sum(p, axis=0, keepdims=True)
    acc0 = jnp.dot(vt_ref[i, a], p.astype(BF16), preferred_element_type=F32)
    return m0, l0, acc0


def _moba_kernel(qa_ref, ka_ref, vt_ref, o_ref, s_ref, acc_ref, *, nblk, hps, kt):
    i = pl.program_id(2)
    blk = MOBA_BLOCK
    dh = ATT_HEAD_DIM
    tk = kt * blk
    last_tile = nblk // kt - 1
    nt = (((1,), (1,)), ((), ()))
    s_slots = (s_ref.at[0], s_ref.at[1])
    def scores_into(slot, t):
        start = pl.multiple_of(jnp.minimum(t, last_tile) * tk, tk)
        for a in range(hps):
            aug = slice(2 * a * dh, (2 * a + 2) * dh)
            s_slots[slot][a] = lax.dot_general(ka_ref[pl.ds(start, tk), aug], qa_ref[:, aug], nt,
                                               preferred_element_type=F32)

    def update(slot, t, state):
        for a in range(hps):
            m, l = state[2 * a:2 * a + 2]
            s = s_slots[slot][a]
            m_new = jnp.maximum(m, jnp.max(s, axis=0, keepdims=True))
            alpha = jnp.exp2(m - m_new)
            p = jnp.exp2(s - m_new)
            state[2 * a:2 * a + 2] = [m_new, alpha * l + jnp.sum(p, axis=0, keepdims=True)]
            v_t = jnp.concatenate([vt_ref[t * kt + c, a] for c in range(kt)], axis=1)
            acc_ref[a] = alpha * acc_ref[a] + jnp.dot(v_t, p.astype(BF16),
                                                      preferred_element_type=F32)
        return state

    own = [_own_scores(i, qa_ref[:, 2 * a * dh:(2 * a + 1) * dh], ka_ref, a) for a in range(hps)]
    scores_into(0, 0)
    init = []
    for a in range(hps):
        m0, l0, acc0 = _own_state(i, own[a], vt_ref, a)
        acc_ref[a] = acc0
        init.extend((m0, l0))


    def pair(t0, state):
        scores_into(1, t0 + 1)
        state = update(0, t0, state)
        scores_into(0, t0 + 2)
        return update(1, jnp.minimum(t0 + 1, last_tile), state)

    ntile = (i + kt - 1) // kt
    quads = ntile // 4
    state = lax.fori_loop(
        0, quads, lambda j, c: tuple(pair(4 * j + 2, pair(4 * j, list(c)))), tuple(init))
    rest = ntile - 4 * quads
    fin = lax.fori_loop(
        0, (rest + 1) // 2, lambda j, c: tuple(pair(4 * quads + 2 * j, list(c))), state)
    for a in range(hps):
        o_t = acc_ref[a] / fin[2 * a + 1]
        o_ref[:, a * dh:(a + 1) * dh] = o_t.T.astype(o_ref.dtype)


MOBA_HEADS_PER_STEP = 2
MOBA_BLOCKS_PER_TILE = 2


def moba_attention(qa, ka, vt, batch, seq):
    nblk = seq // MOBA_BLOCK
    hps, kt = MOBA_HEADS_PER_STEP, MOBA_BLOCKS_PER_TILE
    assert ATT_HEADS % hps == 0 and nblk % kt == 0 and nblk <= ATT_HEAD_DIM
    w = hps * ATT_HEAD_DIM
    return pl.pallas_call(
        functools.partial(_moba_kernel, nblk=nblk, hps=hps, kt=kt),
        out_shape=jax.ShapeDtypeStruct((qa.shape[0], ATT_WIDTH), BF16),
        grid=(batch, ATT_HEADS // hps, nblk),
        in_specs=[pl.BlockSpec((MOBA_BLOCK, 2 * w), lambda b, h, i: (b * nblk + i, h)),
                  pl.BlockSpec((seq, 2 * w), lambda b, h, i: (b, h)),
                  pl.BlockSpec((nblk, hps, ATT_HEAD_DIM, MOBA_BLOCK), lambda b, h, i: (b, h, 0, 0))],
        out_specs=pl.BlockSpec((MOBA_BLOCK, w), lambda b, h, i: (b * nblk + i, h)),
        scratch_shapes=[pltpu.VMEM((2, hps, kt * MOBA_BLOCK, MOBA_BLOCK), F32),
                        pltpu.VMEM((hps, ATT_HEAD_DIM, MOBA_BLOCK), F32)],
        compiler_params=_cparams(("parallel", "parallel", "arbitrary")),
        name="moba_attn",
    )(qa, ka, vt)


def _silu(x):
    return x * (0.5 * jnp.tanh(0.5 * x) + 0.5)


def _ssd_kernel(z0_ref, z1_ref, xs0_ref, xs1_ref, bc_ref, dt_ref, cwx_ref, cbx_ref, cwb_ref, cbb_ref,
                dtb_ref, alog_ref, dsk_ref, nw_ref, e3_ref, o_ref, xext_ref, bext_ref, state_ref,
                y_ref, *, heads):
    c = pl.program_id(1)
    q = SSD_CHUNK
    p = SSD_HEAD_DIM
    n = SSD_STATE
    hpg = heads // SSD_GROUPS
    gw = hpg * p

    @pl.when(c == 0)
    def _():
        xext_ref[0:HALO, :] = jnp.zeros((HALO, xext_ref.shape[1]), F32)
        bext_ref[0:HALO, :] = jnp.zeros((HALO, bext_ref.shape[1]), F32)
        state_ref[...] = jnp.zeros(state_ref.shape, F32)

    def conv_silu(ext_ref, raw, w_ref, b_ref):
        ext_ref[HALO:HALO + q, :] = raw
        acc = b_ref[...]
        for kk in range(SSD_CONV):
            off = HALO - (SSD_CONV - 1) + kk
            acc = acc + ext_ref[pl.ds(off, q), :] * w_ref[kk:kk + 1, :]
        ext_ref[0:HALO, :] = raw[q - HALO:, :]
        return _silu(acc)

    xs_raw = jnp.concatenate([xs0_ref[...], xs1_ref[...]], axis=1)
    xs = conv_silu(xext_ref, xs_raw, cwx_ref, cbx_ref)
    bc = conv_silu(bext_ref, bc_ref[...], cwb_ref, cbb_ref)

    dt = dt_ref[...] + dtb_ref[...]
    dt = jnp.maximum(dt, 0.0) + jnp.log1p(jnp.exp(-jnp.abs(dt)))
    a = -jnp.exp(alog_ref[...])
    da = dt * a
    tri = (lax.broadcasted_iota(jnp.int32, (q, q), 0)
           >= lax.broadcasted_iota(jnp.int32, (q, q), 1))
    acs = jnp.dot(tri.astype(F32), da, preferred_element_type=F32,
                  precision=lax.Precision.HIGHEST)
    acs2 = acs * LOG2E
    acs2_t = acs2.T
    hq = q // 2
    tri_h = tri[:hq, :hq]
    a_last = acs[q - 1:q, :]

    def per_channel(v):
        hi = v.astype(BF16)
        r1 = v - hi.astype(F32)
        mid = r1.astype(BF16)
        lo = (r1 - mid.astype(F32)).astype(BF16)
        return jnp.dot(jnp.concatenate([hi, mid, lo], axis=1), e3_ref[...],
                       preferred_element_type=F32)

    dec_in = per_channel(jnp.exp(acs))
    xdt = xs * per_channel(dt)
    xd = (xs * per_channel(dt * jnp.exp(a_last - acs))).astype(BF16)
    skip = xs * dsk_ref[...]
    first_head = lax.broadcasted_iota(jnp.int32, (q, 2 * p), 1) < p

    def decay_quadrants(cb, h):
        col, rw = acs2[:, h:h + 1], acs2_t[h:h + 1, :]
        m00 = cb[:hq, :hq] * jnp.where(tri_h, jnp.exp2(col[:hq] - rw[:, :hq]), 0.0)
        m10 = cb[hq:, :hq] * jnp.exp2(col[hq:] - rw[:, :hq])
        m11 = cb[hq:, hq:] * jnp.where(tri_h, jnp.exp2(col[hq:] - rw[:, hq:]), 0.0)
        return m00.astype(BF16), jnp.concatenate([m10, m11], axis=1).astype(BF16)

    for g in range(SSD_GROUPS):
        b_g = bc[:, g * n:(g + 1) * n]
        c_g = bc[:, (SSD_GROUPS + g) * n:(SSD_GROUPS + g + 1) * n]
        b_t = b_g.T.astype(BF16)
        c_b = c_g.astype(BF16)
        cb = jnp.dot(c_b, b_t, preferred_element_type=F32)
        gl = slice(g * gw, (g + 1) * gw)
        st_g = state_ref[g]
        y_off = jnp.dot(c_b, st_g.astype(BF16), preferred_element_type=F32)
        for j in range(0, hpg, 2):
            h = g * hpg + j
            pl_ = slice(h * p, (h + 2) * p)
            xp = xdt[:, pl_]
            xa = jnp.where(first_head, xp, 0.0).astype(BF16)
            xb = jnp.where(first_head, 0.0, xp).astype(BF16)
            top_a, bot_a = decay_quadrants(cb, h)
            top_b, bot_b = decay_quadrants(cb, h + 1)
            y_top = (jnp.dot(top_a, xa[:hq], preferred_element_type=F32)
                     + jnp.dot(top_b, xb[:hq], preferred_element_type=F32))
            y_bot = (jnp.dot(bot_a, xa, preferred_element_type=F32)
                     + jnp.dot(bot_b, xb, preferred_element_type=F32))
            y_ref[:, pl_] = (jnp.concatenate([y_top, y_bot], axis=0)
                             + y_off[:, j * p:(j + 2) * p] * dec_in[:, pl_] + skip[:, pl_])
        st_new = jnp.dot(b_t, xd[:, gl], preferred_element_type=F32)
        state_ref[g] = st_g * dec_in[q - 1:q, gl] + st_new

    zz = jnp.concatenate([z0_ref[...], z1_ref[...]], axis=1)
    y = y_ref[...] * _silu(zz)
    for g in range(SSD_GROUPS):
        yg = y[:, g * gw:(g + 1) * gw]
        yg = yg * lax.rsqrt(jnp.mean(yg * yg, axis=-1, keepdims=True) + EPS)
        o_ref[:, g * gw:(g + 1) * gw] = (yg * nw_ref[:, g * gw:(g + 1) * gw]).astype(o_ref.dtype)


def ssd_mixer(proj, dtp, z_col, xs_col, bc_col, dt_col, conv_w, conv_b, dt_bias, a_log, d_skip,
              norm_w, batch, seq, width):
    m = proj.shape[0]
    heads = width // SSD_HEAD_DIM
    nc = seq // SSD_CHUNK
    bcw = 2 * SSD_GROUPS * SSD_STATE
    assert heads <= LANES and (heads // SSD_GROUPS) % 2 == 0 and 2 * SSD_HEAD_DIM == LANES
    pad = lambda v: jnp.pad(v.astype(F32), (0, LANES - heads)).reshape(1, LANES)
    row = lambda w, col: pl.BlockSpec((SSD_CHUNK, w), lambda b, c: (b * nc + c, col))
    full = lambda r, w: pl.BlockSpec((r, w), lambda b, c: (0, 0))
    head_of = jnp.arange(width, dtype=jnp.int32) // SSD_HEAD_DIM
    e1 = (jnp.arange(LANES, dtype=jnp.int32)[:, None] == head_of[None, :]).astype(BF16)
    e3 = jnp.concatenate([e1, e1, e1], axis=0)
    return pl.pallas_call(
        functools.partial(_ssd_kernel, heads=heads),
        out_shape=jax.ShapeDtypeStruct((m, width), BF16),
        grid=(batch, nc),
        in_specs=[row(width // 2, z_col), row(width // 2, z_col + 1),
                  row(width // 2, xs_col), row(width // 2, xs_col + 1),
                  row(bcw, bc_col), row(LANES, dt_col),
                  full(SSD_CONV, width), full(1, width), full(SSD_CONV, bcw), full(1, bcw),
                  full(1, LANES), full(1, LANES), full(1, width), full(1, width),
                  full(3 * LANES, width)],
        out_specs=pl.BlockSpec((SSD_CHUNK, width), lambda b, c: (b * nc + c, 0)),
        scratch_shapes=[pltpu.VMEM((HALO + SSD_CHUNK, width), F32),
                        pltpu.VMEM((HALO + SSD_CHUNK, bcw), F32),
                        pltpu.VMEM((SSD_GROUPS, SSD_STATE, width // SSD_GROUPS), F32),
                        pltpu.VMEM((SSD_CHUNK, width), F32)],
        compiler_params=_cparams(("parallel", "arbitrary")),
        name="ssd",
    )(proj, proj, proj, proj, proj, dtp, conv_w[:, :width], conv_b[:width].reshape(1, width),
      conv_w[:, width:], conv_b[width:].reshape(1, bcw), pad(dt_bias), pad(a_log),
      jnp.repeat(d_skip.astype(F32), SSD_HEAD_DIM).reshape(1, width), norm_w.reshape(1, width), e3)


def _mem_attn_kernel(q_ref, k_ref, v_ref, o_ref):
    scale = MEM_HEAD_DIM ** -0.5
    nt = (((1,), (1,)), ((), ()))
    for h in range(MEM_HEADS):
        sl = slice(h * MEM_HEAD_DIM, (h + 1) * MEM_HEAD_DIM)
        q = q_ref[:, sl].astype(BF16)
        s = lax.dot_general(q, k_ref[:, sl], nt, preferred_element_type=F32) * scale
        p = jnp.exp(s - jnp.max(s, axis=1, keepdims=True))
        l = jnp.sum(p, axis=1, keepdims=True)
        o = jnp.dot(p.astype(BF16), v_ref[:, sl], preferred_element_type=F32)
        o_ref[:, sl] = (o / l).astype(o_ref.dtype)


def mem_attention(qp, kv, batch, seq, tq):
    w = MEM_HEADS * MEM_HEAD_DIM
    mem_len = kv.shape[0] // batch
    nq = seq // tq
    return pl.pallas_call(
        _mem_attn_kernel,
        out_shape=jax.ShapeDtypeStruct((batch * seq, w), BF16),
        grid=(batch, nq),
        in_specs=[pl.BlockSpec((tq, w), lambda b, i: (b * nq + i, 0)),
                  pl.BlockSpec((mem_len, w), lambda b, i: (b, 0)),
                  pl.BlockSpec((mem_len, w), lambda b, i: (b, 1))],
        out_specs=pl.BlockSpec((tq, w), lambda b, i: (b * nq + i, 0)),
        compiler_params=_cparams(("parallel", "parallel")),
        name="mem_attn",
    )(qp, kv, kv)


def _sigmoid(x):
    return 1.0 / (1.0 + jnp.exp(-x))


def _merge_kernel(ya_ref, ys_ref, ym_ref, ga_ref, gs_ref, gm_ref, wa_ref, ws_ref, wm_ref, o_ref):
    acc = _sigmoid(ga_ref[...]) * jnp.dot(ya_ref[...], wa_ref[...], preferred_element_type=F32)
    acc += _sigmoid(gs_ref[...]) * jnp.dot(ys_ref[...], ws_ref[...], preferred_element_type=F32)
    acc += _sigmoid(gm_ref[...]) * jnp.dot(ym_ref[...], wm_ref[...], preferred_element_type=F32)
    o_ref[...] = acc.astype(o_ref.dtype)


def branch_merge(ya, ys, ym, proj, gate_col, wa, ws, wm, tm, tn, layer=None):
    m = ya.shape[0]
    d = wa.shape[-1]
    per = d // tn
    g0 = gate_col // tn
    yspec = lambda w: pl.BlockSpec((tm, w), lambda i, j: (i, 0))
    gspec = lambda k: pl.BlockSpec((tm, tn), lambda i, j: (i, g0 + k * per + j))
    wspec = lambda w: _wspec(w, layer, (w.shape[-2], tn), lambda i, j: (0, j))
    return pl.pallas_call(
        _merge_kernel,
        out_shape=jax.ShapeDtypeStruct((m, d), BF16),
        grid=(m // tm, per),
        in_specs=[yspec(ya.shape[1]), yspec(ys.shape[1]), yspec(ym.shape[1]),
                  gspec(0), gspec(1), gspec(2),
                  wspec(wa), wspec(ws), wspec(wm)],
        out_specs=pl.BlockSpec((tm, tn), lambda i, j: (i, j)),
        compiler_params=_cparams(("parallel", "arbitrary")),
        name="branch_merge",
    )(ya, ys, ym, proj, proj, proj, wa, ws, wm)


def _out_residual_kernel(a_ref, w_ref, x_ref, g_ref, o_ref):
    y = jnp.dot(a_ref[...], w_ref[...], preferred_element_type=F32)
    o_ref[...] = x_ref[...] + _rms(y, g_ref[...])


def out_residual(a, w, x, gain, tm, layer=None):
    m, d = x.shape
    return pl.pallas_call(
        _out_residual_kernel,
        out_shape=jax.ShapeDtypeStruct((m, d), F32),
        grid=(m // tm,),
        in_specs=[pl.BlockSpec((tm, a.shape[1]), lambda i: (i, 0)),
                  _wspec(w, layer, w.shape[-2:], lambda i: (0, 0)),
                  pl.BlockSpec((tm, d), lambda i: (i, 0)),
                  pl.BlockSpec((1, d), lambda i: (0, 0))],
        out_specs=pl.BlockSpec((tm, d), lambda i: (i, 0)),
        compiler_params=_cparams(("parallel",)),
        name="out_residual",
    )(a, w, x, gain.reshape(1, d))


def _gelu_tanh(x):
    return 0.5 * x * (1.0 + jnp.tanh(0.7978845608028654 * (x + 0.044715 * x * x * x)))


def _ffn_kernel(x_ref, xh_ref, gpre_ref, wa_ref, wg_ref, cwa_ref, cwg_ref, cba_ref, cbg_ref,
                wd_ref, gpost_ref, o_ref, hn_ref, u_ref, acc_ref, *, tm, seq, nf):
    i = pl.program_id(0)
    j = pl.program_id(1)
    seq_start = (i * tm) % seq == 0

    def up(a_ref, g_ref):
        for k, w_ref in enumerate((a_ref, g_ref)):
            u = jnp.dot(hn_ref[...], w_ref[...], preferred_element_type=F32)
            u_ref[k, 0:HALO, :] = jnp.where(seq_start, 0.0, u[0:HALO, :])
            u_ref[k, HALO:HALO + tm, :] = u[HALO:, :]

    def conv(k, cw_ref, cb_ref):
        out = cb_ref[...]
        for kk in range(FFN_CONV):
            off = HALO - (FFN_CONV - 1) + kk
            out = out + u_ref[k, pl.ds(off, tm), :] * cw_ref[kk:kk + 1, :]
        return out

    def consume():
        a = conv(0, cwa_ref, cba_ref)
        g = conv(1, cwg_ref, cbg_ref)
        act = (_gelu_tanh(a) * g).astype(BF16)
        acc_ref[...] += jnp.dot(act, wd_ref[...], preferred_element_type=F32)

    @pl.when(j == 0)
    def _():
        hn_ref[0:HALO, :] = _rms(xh_ref[...], gpre_ref[...]).astype(BF16)
        hn_ref[HALO:HALO + tm, :] = _rms(x_ref[...], gpre_ref[...]).astype(BF16)
        acc_ref[...] = jnp.zeros(acc_ref.shape, F32)

    up(wa_ref, wg_ref)
    consume()

    @pl.when(j == nf - 1)
    def _():
        o_ref[...] = x_ref[...] + _rms(acc_ref[...], gpost_ref[...])


def conv_glu_ffn(x, gpre, w_up, conv_w, conv_b, w_down, gpost, seq, tm, tf, layer=None):
    m, d = x.shape
    dff = w_down.shape[-2]
    nf = dff // tf
    hb = tm // HALO
    vec = lambda: pl.BlockSpec((1, d), lambda i, j: (0, 0))
    return pl.pallas_call(
        functools.partial(_ffn_kernel, tm=tm, seq=seq, nf=nf),
        out_shape=jax.ShapeDtypeStruct((m, d), F32),
        grid=(m // tm, nf),
        in_specs=[pl.BlockSpec((tm, d), lambda i, j: (i, 0)),
                  pl.BlockSpec((HALO, d), lambda i, j: (jnp.maximum(i * hb - 1, 0), 0)),
                  vec(),
                  _wspec(w_up, layer, (d, tf), lambda i, j: (0, j)),
                  _wspec(w_up, layer, (d, tf), lambda i, j: (0, nf + j)),
                  pl.BlockSpec((FFN_CONV, tf), lambda i, j: (0, j)),
                  pl.BlockSpec((FFN_CONV, tf), lambda i, j: (0, nf + j)),
                  pl.BlockSpec((1, tf), lambda i, j: (0, j)),
                  pl.BlockSpec((1, tf), lambda i, j: (0, nf + j)),
                  _wspec(w_down, layer, (tf, d), lambda i, j: (j, 0)),
                  vec()],
        out_specs=pl.BlockSpec((tm, d), lambda i, j: (i, 0)),
        scratch_shapes=[pltpu.VMEM((HALO + tm, d), BF16),
                        pltpu.VMEM((2, HALO + tm, tf), F32),
                        pltpu.VMEM((tm, d), F32)],
        compiler_params=_cparams(("parallel", "arbitrary")),
        name="conv_glu_ffn",
    )(x, x, gpre.reshape(1, d), w_up, w_up, conv_w, conv_w, conv_b.reshape(1, -1),
      conv_b.reshape(1, -1), w_down, gpost.reshape(1, d))


def _rope_tables(seq):
    half = ATT_HEAD_DIM // 2
    inv = ROPE_THETA ** (-jnp.arange(half, dtype=F32) / half)
    ang = jnp.arange(seq, dtype=F32)[:, None] * inv[None, :]
    cos, sin = jnp.cos(ang), jnp.sin(ang)
    return jnp.concatenate([cos, cos], axis=1), jnp.concatenate([-sin, sin], axis=1)


def kernel(x, mem, norm_mix_pre, norm_mix_post, norm_ffn_pre, norm_ffn_post, norm_mem, w_in,
           conv_ssd_w, conv_ssd_b, dt_bias, a_log, d_skip, ssd_norm, w_mem_kv, w_br_attn, w_br_ssd,
           w_br_mem, w_out, w_up, conv_ffn_w, conv_ffn_b, w_down):
    batch, seq, d = x.shape
    depth = w_in.shape[0]
    mem_len = mem.shape[1]
    ssd_w = w_br_ssd.shape[1]
    heads = ssd_w // SSD_HEAD_DIM
    bcw = 2 * SSD_GROUPS * SSD_STATE
    mem_w = MEM_HEADS * MEM_HEAD_DIM
    assert seq % MOBA_BLOCK == 0 and seq % SSD_CHUNK == 0 and d == ssd_w

    o_q, o_z = 0, 3 * ATT_WIDTH
    o_xs = o_z + ssd_w
    o_bc = o_xs + ssd_w
    o_dt = o_bc + bcw
    o_qm = o_dt + heads
    o_g = o_qm + mem_w

    half = ssd_w // 2
    assert o_z % half == 0 and o_xs % half == 0 and o_bc % bcw == 0 and o_q % ATT_WIDTH == 0

    cos, sin = _rope_tables(seq)
    xf = x.reshape(batch * seq, d)
    memf = mem.reshape(batch * mem_len, d)

    w_all = w_in.astype(BF16)
    w_gate = w_all[:, :, o_g:]
    w_b = jnp.concatenate([w_all[:, :, o_qm:o_g], w_all[:, :, o_dt:o_qm],
                           jnp.zeros((depth, d, LANES - heads), BF16)], axis=2)
    w_kv, w_ba, w_bs, w_bm, w_o, w_u, w_d = (
        t.astype(BF16) for t in (w_mem_kv, w_br_attn, w_br_ssd, w_br_mem, w_out, w_up, w_down))

    for l in range(depth):
        proj = norm_matmul(xf, norm_mix_pre[l], w_all, F32, tm=1024, tn=1024, layer=l, n=o_dt)
        gates = norm_matmul(xf, norm_mix_pre[l], w_gate, F32, tm=1024, tn=1024, layer=l)
        projb = norm_matmul(xf, norm_mix_pre[l], w_b, F32, tm=1024, tn=w_b.shape[2], layer=l)

        qa, ka, vt = rope_route(proj, cos, sin, o_q // ATT_WIDTH, seq)
        y_a = moba_attention(qa, ka, vt, batch, seq)
        y_s = ssd_mixer(proj, projb, o_z // half, o_xs // half, o_bc // bcw, mem_w // LANES,
                        conv_ssd_w[l], conv_ssd_b[l], dt_bias[l], a_log[l], d_skip[l], ssd_norm[l],
                        batch, seq, ssd_w)
        kv = norm_matmul(memf, norm_mem[l], w_kv, BF16, tm=batch * mem_len, tn=512, layer=l)
        y_m = mem_attention(projb, kv, batch, seq, tq=512)
        merged = branch_merge(y_a, y_s, y_m, gates, 0, w_ba, w_bs, w_bm, tm=1024, tn=512, layer=l)
        xf = out_residual(merged, w_o, xf, norm_mix_post[l], tm=512, layer=l)
        xf = conv_glu_ffn(xf, norm_ffn_pre[l], w_u, conv_ffn_w[l], conv_ffn_b[l], w_d,
                          norm_ffn_post[l], seq, tm=512, tf=512, layer=l)
    return xf.reshape(batch, seq, d)
```

```python
import functools

import jax
import jax.numpy as jnp
from jax import lax
from jax.experimental import pallas as pl
from jax.experimental.pallas import tpu as pltpu

F32 = jnp.float32
BF16 = jnp.bfloat16

EPS = 1e-6
NEG = -1e30
ROPE_THETA = 10000.0

ATT_HEADS = 8
ATT_HEAD_DIM = 128
ATT_WIDTH = ATT_HEADS * ATT_HEAD_DIM
MOBA_BLOCK = 256
MOBA_TOPK = 3
LOG2E = 1.4426950408889634
Q_PRESCALE = ATT_HEAD_DIM ** -0.5 * LOG2E

SSD_HEAD_DIM = 64
SSD_GROUPS = 4
SSD_STATE = 128
SSD_CONV = 4
SSD_CHUNK = 256

MEM_HEADS = 4
MEM_HEAD_DIM = 256
FFN_CONV = 3

LANES = 128
HALO = 16
VMEM_LIMIT = 56 * 1024 * 1024


def _cparams(semantics, vmem=VMEM_LIMIT):
    return pltpu.CompilerParams(dimension_semantics=semantics, vmem_limit_bytes=vmem)


def _rms(x, gain):
    return x * lax.rsqrt(jnp.mean(x * x, axis=-1, keepdims=True) + EPS) * gain


def _norm_matmul_kernel(x_ref, g_ref, w_ref, o_ref, hn_ref):
    @pl.when(pl.program_id(1) == 0)
    def _():
        hn_ref[...] = _rms(x_ref[...], g_ref[...]).astype(BF16)

    o_ref[...] = jnp.dot(hn_ref[...], w_ref[...], preferred_element_type=F32).astype(o_ref.dtype)


def _wspec(w, layer, block, index):
    if w.ndim == 2:
        return pl.BlockSpec(block, index)
    return pl.BlockSpec((None,) + tuple(block), lambda *g: (layer,) + tuple(index(*g)))


def norm_matmul(x, gain, w, out_dtype, tm, tn, layer=None, n=None):
    m, d = x.shape
    n = w.shape[-1] if n is None else n
    return pl.pallas_call(
        _norm_matmul_kernel,
        out_shape=jax.ShapeDtypeStruct((m, n), out_dtype),
        grid=(m // tm, n // tn),
        in_specs=[pl.BlockSpec((tm, d), lambda i, j: (i, 0)),
                  pl.BlockSpec((1, d), lambda i, j: (0, 0)),
                  _wspec(w, layer, (d, tn), lambda i, j: (0, j))],
        out_specs=pl.BlockSpec((tm, tn), lambda i, j: (i, j)),
        scratch_shapes=[pltpu.VMEM((tm, d), BF16)],
        compiler_params=_cparams(("parallel", "arbitrary")),
        name="norm_matmul",
    )(x, gain.reshape(1, d), w)


def _route_bias(q, km, blk_idx, nblk):
    blk = MOBA_BLOCK
    nt = (((1,), (1,)), ((), ()))
    km_hi = km.astype(BF16)
    r1 = km - km_hi.astype(F32)
    km_mid = r1.astype(BF16)
    km_lo = (r1 - km_mid.astype(F32)).astype(BF16)
    gate = (lax.dot_general(km_hi, q, nt, preferred_element_type=F32)
            + lax.dot_general(km_mid, q, nt, preferred_element_type=F32)
            + lax.dot_general(km_lo, q, nt, preferred_element_type=F32))
    row = lax.broadcasted_iota(jnp.int32, (nblk, blk), 0).astype(F32)
    past = row < blk_idx.astype(F32)
    g = jnp.where(past, gate, NEG)
    bias = jnp.full((nblk, blk), NEG, F32)
    for _ in range(min(MOBA_TOPK, max(nblk - 1, 1))):
        mx = jnp.max(g, axis=0, keepdims=True)
        first = jnp.min(jnp.where(g == mx, row, float(nblk)), axis=0, keepdims=True)
        pick = row == first
        bias = jnp.where(pick, 0.0, bias)
        g = jnp.where(pick, -jnp.inf, g)
    bias = jnp.where(past, bias, NEG)
    return jnp.concatenate([bias, jnp.zeros((ATT_HEAD_DIM - nblk, blk), F32)], axis=0).T


def _rope_kernel(q_ref, k_ref, v_ref, cos_ref, sin_ref, qa_ref, ka_ref, vt_ref, km_ref, *, per_seq):
    cos = cos_ref[...]
    sin = sin_ref[...]
    half = ATT_HEAD_DIM // 2
    dh = ATT_HEAD_DIM
    blk_idx = pl.program_id(0) % per_seq

    @pl.when(pl.program_id(0) == 0)
    def _():
        km_ref[...] = jnp.zeros(km_ref.shape, F32)

    lane = lax.broadcasted_iota(jnp.int32, (MOBA_BLOCK, dh), 1)
    onehot = jnp.where(lane == blk_idx, 1.0, 0.0).astype(BF16)
    this_blk = lax.broadcasted_iota(jnp.int32, (per_seq, dh), 0) == blk_idx
    for h in range(ATT_HEADS):
        sl = slice(h * dh, (h + 1) * dh)
        q = q_ref[:, sl]
        k = k_ref[:, sl]
        qr = q * cos + pltpu.roll(q, half, 1) * sin
        kr = k * cos + pltpu.roll(k, half, 1) * sin
        qb = (qr * Q_PRESCALE).astype(BF16)
        km = km_ref[:, sl]
        qa_ref[:, 2 * h * dh:(2 * h + 1) * dh] = qb
        qa_ref[:, (2 * h + 1) * dh:(2 * h + 2) * dh] = _route_bias(qb, km, blk_idx, per_seq).astype(BF16)
        ka_ref[:, 2 * h * dh:(2 * h + 1) * dh] = kr.astype(BF16)
        ka_ref[:, (2 * h + 1) * dh:(2 * h + 2) * dh] = onehot
        km_ref[:, sl] = jnp.where(this_blk, jnp.mean(kr, axis=0, keepdims=True), km)
        vt_ref[0, h] = v_ref[:, sl].T.astype(BF16)


def rope_route(proj, cos, sin, q_col, seq):
    m = proj.shape[0]
    nblk = m // MOBA_BLOCK
    per_seq = seq // MOBA_BLOCK
    assert per_seq <= ATT_HEAD_DIM
    row = lambda c: pl.BlockSpec((MOBA_BLOCK, ATT_WIDTH), lambda i: (i, c))
    tab = pl.BlockSpec((MOBA_BLOCK, ATT_HEAD_DIM), lambda i: (i % per_seq, 0))
    aug = pl.BlockSpec((MOBA_BLOCK, 2 * ATT_WIDTH), lambda i: (i, 0))
    return pl.pallas_call(
        functools.partial(_rope_kernel, per_seq=per_seq),
        out_shape=(jax.ShapeDtypeStruct((m, 2 * ATT_WIDTH), BF16),
                   jax.ShapeDtypeStruct((m, 2 * ATT_WIDTH), BF16),
                   jax.ShapeDtypeStruct((nblk, ATT_HEADS, ATT_HEAD_DIM, MOBA_BLOCK), BF16)),
        grid=(nblk,),
        in_specs=[row(q_col), row(q_col + 1), row(q_col + 2), tab, tab],
        out_specs=(aug, aug,
                   pl.BlockSpec((1, ATT_HEADS, ATT_HEAD_DIM, MOBA_BLOCK), lambda i: (i, 0, 0, 0))),
        scratch_shapes=[pltpu.VMEM((per_seq, ATT_WIDTH), F32)],
        compiler_params=_cparams(("arbitrary",)),
        name="rope_route",
    )(proj, proj, proj, cos, sin)


def _own_scores(i, q, ka_ref, a):
    dh = ATT_HEAD_DIM
    start = pl.multiple_of(i * MOBA_BLOCK, MOBA_BLOCK)
    k_own = ka_ref[pl.ds(start, MOBA_BLOCK), 2 * a * dh:(2 * a + 1) * dh]
    return lax.dot_general(k_own, q, (((1,), (1,)), ((), ())), preferred_element_type=F32)


def _own_state(i, s, vt_ref, a):
    blk = MOBA_BLOCK
    key = lax.broadcasted_iota(jnp.int32, (blk, blk), 0)
    qry = lax.broadcasted_iota(jnp.int32, (blk, blk), 1)
    s = jnp.where(key <= qry, s, NEG)
    m0 = jnp.max(s, axis=0, keepdims=True)
    p = jnp.exp2(s - m0)
    l0 = jnp.sum(p, axis=0, keepdims=True)
    acc0 = jnp.dot(vt_ref[i, a], p.astype(BF16), preferred_element_type=F32)
    return m0, l0, acc0


def _moba_kernel(qa_ref, ka_ref, vt_ref, o_ref, s_ref, acc_ref, *, nblk, hps, kt):
    i = pl.program_id(2)
    blk = MOBA_BLOCK
    dh = ATT_HEAD_DIM
    tk = kt * blk
    last_tile = nblk // kt - 1
    nt = (((1,), (1,)), ((), ()))
    s_slots = (s_ref.at[0], s_ref.at[1])
    def scores_into(slot, t):
        start = pl.multiple_of(jnp.minimum(t, last_tile) * tk, tk)
        for a in range(hps):
            aug = slice(2 * a * dh, (2 * a + 2) * dh)
            s_slots[slot][a] = lax.dot_general(ka_ref[pl.ds(start, tk), aug], qa_ref[:, aug], nt,
                                               preferred_element_type=F32)

    def update(slot, t, state):
        for a in range(hps):
            m, l = state[2 * a:2 * a + 2]
            s = s_slots[slot][a]
            m_new = jnp.maximum(m, jnp.max(s, axis=0, keepdims=True))
            alpha = jnp.exp2(m - m_new)
            p = jnp.exp2(s - m_new)
            state[2 * a:2 * a + 2] = [m_new, alpha * l + jnp.sum(p, axis=0, keepdims=True)]
            v_t = jnp.concatenate([vt_ref[t * kt + c, a] for c in range(kt)], axis=1)
            acc_ref[a] = alpha * acc_ref[a] + jnp.dot(v_t, p.astype(BF16),
                                                      preferred_element_type=F32)
        return state

    own = [_own_scores(i, qa_ref[:, 2 * a * dh:(2 * a + 1) * dh], ka_ref, a) for a in range(hps)]
    scores_into(0, 0)
    init = []
    for a in range(hps):
        m0, l0, acc0 = _own_state(i, own[a], vt_ref, a)
        acc_ref[a] = acc0
        init.extend((m0, l0))


    def pair(t0, state):
        scores_into(1, t0 + 1)
        state = update(0, t0, state)
        scores_into(0, t0 + 2)
        return update(1, jnp.minimum(t0 + 1, last_tile), state)

    ntile = (i + kt - 1) // kt
    quads = ntile // 4
    state = lax.fori_loop(
        0, quads, lambda j, c: tuple(pair(4 * j + 2, pair(4 * j, list(c)))), tuple(init))
    rest = ntile - 4 * quads
    fin = lax.fori_loop(
        0, (rest + 1) // 2, lambda j, c: tuple(pair(4 * quads + 2 * j, list(c))), state)
    for a in range(hps):
        o_t = acc_ref[a] / fin[2 * a + 1]
        o_ref[:, a * dh:(a + 1) * dh] = o_t.T.astype(o_ref.dtype)


MOBA_HEADS_PER_STEP = 2
MOBA_BLOCKS_PER_TILE = 2


def moba_attention(qa, ka, vt, batch, seq):
    nblk = seq // MOBA_BLOCK
    hps, kt = MOBA_HEADS_PER_STEP, MOBA_BLOCKS_PER_TILE
    assert ATT_HEADS % hps == 0 and nblk % kt == 0 and nblk <= ATT_HEAD_DIM
    w = hps * ATT_HEAD_DIM
    return pl.pallas_call(
        functools.partial(_moba_kernel, nblk=nblk, hps=hps, kt=kt),
        out_shape=jax.ShapeDtypeStruct((qa.shape[0], ATT_WIDTH), BF16),
        grid=(batch, ATT_HEADS // hps, nblk),
        in_specs=[pl.BlockSpec((MOBA_BLOCK, 2 * w), lambda b, h, i: (b * nblk + i, h)),
                  pl.BlockSpec((seq, 2 * w), lambda b, h, i: (b, h)),
                  pl.BlockSpec((nblk, hps, ATT_HEAD_DIM, MOBA_BLOCK), lambda b, h, i: (b, h, 0, 0))],
        out_specs=pl.BlockSpec((MOBA_BLOCK, w), lambda b, h, i: (b * nblk + i, h)),
        scratch_shapes=[pltpu.VMEM((2, hps, kt * MOBA_BLOCK, MOBA_BLOCK), F32),
                        pltpu.VMEM((hps, ATT_HEAD_DIM, MOBA_BLOCK), F32)],
        compiler_params=_cparams(("parallel", "parallel", "arbitrary")),
        name="moba_attn",
    )(qa, ka, vt)


def _silu(x):
    return x * (0.5 * jnp.tanh(0.5 * x) + 0.5)


def _ssd_kernel(z0_ref, z1_ref, xs0_ref, xs1_ref, bc_ref, dt_ref, cwx_ref, cbx_ref, cwb_ref, cbb_ref,
                dtb_ref, alog_ref, dsk_ref, nw_ref, e3_ref, o_ref, xext_ref, bext_ref, state_ref,
                y_ref, *, heads):
    c = pl.program_id(1)
    q = SSD_CHUNK
    p = SSD_HEAD_DIM
    n = SSD_STATE
    hpg = heads // SSD_GROUPS
    gw = hpg * p

    @pl.when(c == 0)
    def _():
        xext_ref[0:HALO, :] = jnp.zeros((HALO, xext_ref.shape[1]), F32)
        bext_ref[0:HALO, :] = jnp.zeros((HALO, bext_ref.shape[1]), F32)
        state_ref[...] = jnp.zeros(state_ref.shape, F32)

    def conv_silu(ext_ref, raw, w_ref, b_ref):
        ext_ref[HALO:HALO + q, :] = raw
        acc = b_ref[...]
        for kk in range(SSD_CONV):
            off = HALO - (SSD_CONV - 1) + kk
            acc = acc + ext_ref[pl.ds(off, q), :] * w_ref[kk:kk + 1, :]
        ext_ref[0:HALO, :] = raw[q - HALO:, :]
        return _silu(acc)

    xs_raw = jnp.concatenate([xs0_ref[...], xs1_ref[...]], axis=1)
    xs = conv_silu(xext_ref, xs_raw, cwx_ref, cbx_ref)
    bc = conv_silu(bext_ref, bc_ref[...], cwb_ref, cbb_ref)

    dt = dt_ref[...] + dtb_ref[...]
    dt = jnp.maximum(dt, 0.0) + jnp.log1p(jnp.exp(-jnp.abs(dt)))
    a = -jnp.exp(alog_ref[...])
    da = dt * a
    tri = (lax.broadcasted_iota(jnp.int32, (q, q), 0)
           >= lax.broadcasted_iota(jnp.int32, (q, q), 1))
    acs = jnp.dot(tri.astype(F32), da, preferred_element_type=F32,
                  precision=lax.Precision.HIGHEST)
    acs2 = acs * LOG2E
    acs2_t = acs2.T
    hq = q // 2
    tri_h = tri[:hq, :hq]
    a_last = acs[q - 1:q, :]

    def per_channel(v):
        hi = v.astype(BF16)
        r1 = v - hi.astype(F32)
        mid = r1.astype(BF16)
        lo = (r1 - mid.astype(F32)).astype(BF16)
        return jnp.dot(jnp.concatenate([hi, mid, lo], axis=1), e3_ref[...],
                       preferred_element_type=F32)

    dec_in = per_channel(jnp.exp(acs))
    xdt = xs * per_channel(dt)
    xd = (xs * per_channel(dt * jnp.exp(a_last - acs))).astype(BF16)
    skip = xs * dsk_ref[...]
    first_head = lax.broadcasted_iota(jnp.int32, (q, 2 * p), 1) < p

    def decay_quadrants(cb, h):
        col, rw = acs2[:, h:h + 1], acs2_t[h:h + 1, :]
        m00 = cb[:hq, :hq] * jnp.where(tri_h, jnp.exp2(col[:hq] - rw[:, :hq]), 0.0)
        m10 = cb[hq:, :hq] * jnp.exp2(col[hq:] - rw[:, :hq])
        m11 = cb[hq:, hq:] * jnp.where(tri_h, jnp.exp2(col[hq:] - rw[:, hq:]), 0.0)
        return m00.astype(BF16), jnp.concatenate([m10, m11], axis=1).astype(BF16)

    b_ts, c_bs, y_offs = [], [], []
    for g in range(SSD_GROUPS):
        b_t = bc[:, g * n:(g + 1) * n].T.astype(BF16)
        c_b = bc[:, (SSD_GROUPS + g) * n:(SSD_GROUPS + g + 1) * n].astype(BF16)
        gl = slice(g * gw, (g + 1) * gw)
        st_g = state_ref[g]
        y_offs.append(jnp.dot(c_b, st_g.astype(BF16), preferred_element_type=F32))
        st_new = jnp.dot(b_t, xd[:, gl], preferred_element_type=F32)
        state_ref[g] = st_g * dec_in[q - 1:q, gl] + st_new
        b_ts.append(b_t)
        c_bs.append(c_b)

    for g in range(SSD_GROUPS):
        cb = jnp.dot(c_bs[g], b_ts[g], preferred_element_type=F32)
        y_off = y_offs[g]
        for j in range(0, hpg, 2):
            h = g * hpg + j
            pl_ = slice(h * p, (h + 2) * p)
            xp = xdt[:, pl_]
            xa = jnp.where(first_head, xp, 0.0).astype(BF16)
            xb = jnp.where(first_head, 0.0, xp).astype(BF16)
            top_a, bot_a = decay_quadrants(cb, h)
            top_b, bot_b = decay_quadrants(cb, h + 1)
            y_top = (jnp.dot(top_a, xa[:hq], preferred_element_type=F32)
                     + jnp.dot(top_b, xb[:hq], preferred_element_type=F32))
            y_bot = (jnp.dot(bot_a, xa, preferred_element_type=F32)
                     + jnp.dot(bot_b, xb, preferred_element_type=F32))
            y_ref[:, pl_] = (jnp.concatenate([y_top, y_bot], axis=0)
                             + y_off[:, j * p:(j + 2) * p] * dec_in[:, pl_] + skip[:, pl_])

    zz = jnp.concatenate([z0_ref[...], z1_ref[...]], axis=1)
    y = y_ref[...] * _silu(zz)
    for g in range(SSD_GROUPS):
        yg = y[:, g * gw:(g + 1) * gw]
        yg = yg * lax.rsqrt(jnp.mean(yg * yg, axis=-1, keepdims=True) + EPS)
        o_ref[:, g * gw:(g + 1) * gw] = (yg * nw_ref[:, g * gw:(g + 1) * gw]).astype(o_ref.dtype)


def ssd_mixer(proj, dtp, z_col, xs_col, bc_col, dt_col, conv_w, conv_b, dt_bias, a_log, d_skip,
              norm_w, batch, seq, width):
    m = proj.shape[0]
    heads = width // SSD_HEAD_DIM
    nc = seq // SSD_CHUNK
    bcw = 2 * SSD_GROUPS * SSD_STATE
    assert heads <= LANES and (heads // SSD_GROUPS) % 2 == 0 and 2 * SSD_HEAD_DIM == LANES
    pad = lambda v: jnp.pad(v.astype(F32), (0, LANES - heads)).reshape(1, LANES)
    row = lambda w, col: pl.BlockSpec((SSD_CHUNK, w), lambda b, c: (b * nc + c, col))
    full = lambda r, w: pl.BlockSpec((r, w), lambda b, c: (0, 0))
    head_of = jnp.arange(width, dtype=jnp.int32) // SSD_HEAD_DIM
    e1 = (jnp.arange(LANES, dtype=jnp.int32)[:, None] == head_of[None, :]).astype(BF16)
    e3 = jnp.concatenate([e1, e1, e1], axis=0)
    return pl.pallas_call(
        functools.partial(_ssd_kernel, heads=heads),
        out_shape=jax.ShapeDtypeStruct((m, width), BF16),
        grid=(batch, nc),
        in_specs=[row(width // 2, z_col), row(width // 2, z_col + 1),
                  row(width // 2, xs_col), row(width // 2, xs_col + 1),
                  row(bcw, bc_col), row(LANES, dt_col),
                  full(SSD_CONV, width), full(1, width), full(SSD_CONV, bcw), full(1, bcw),
                  full(1, LANES), full(1, LANES), full(1, width), full(1, width),
                  full(3 * LANES, width)],
        out_specs=pl.BlockSpec((SSD_CHUNK, width), lambda b, c: (b * nc + c, 0)),
        scratch_shapes=[pltpu.VMEM((HALO + SSD_CHUNK, width), F32),
                        pltpu.VMEM((HALO + SSD_CHUNK, bcw), F32),
                        pltpu.VMEM((SSD_GROUPS, SSD_STATE, width // SSD_GROUPS), F32),
                        pltpu.VMEM((SSD_CHUNK, width), F32)],
        compiler_params=_cparams(("parallel", "arbitrary")),
        name="ssd",
    )(proj, proj, proj, proj, proj, dtp, conv_w[:, :width], conv_b[:width].reshape(1, width),
      conv_w[:, width:], conv_b[width:].reshape(1, bcw), pad(dt_bias), pad(a_log),
      jnp.repeat(d_skip.astype(F32), SSD_HEAD_DIM).reshape(1, width), norm_w.reshape(1, width), e3)


def _mem_attn_kernel(q_ref, k_ref, v_ref, o_ref):
    scale = MEM_HEAD_DIM ** -0.5
    nt = (((1,), (1,)), ((), ()))
    for h in range(MEM_HEADS):
        sl = slice(h * MEM_HEAD_DIM, (h + 1) * MEM_HEAD_DIM)
        q = q_ref[:, sl].astype(BF16)
        s = lax.dot_general(q, k_ref[:, sl], nt, preferred_element_type=F32) * scale
        p = jnp.exp(s - jnp.max(s, axis=1, keepdims=True))
        l = jnp.sum(p, axis=1, keepdims=True)
        o = jnp.dot(p.astype(BF16), v_ref[:, sl], preferred_element_type=F32)
        o_ref[:, sl] = (o / l).astype(o_ref.dtype)


def mem_attention(qp, kv, batch, seq, tq):
    w = MEM_HEADS * MEM_HEAD_DIM
    mem_len = kv.shape[0] // batch
    nq = seq // tq
    return pl.pallas_call(
        _mem_attn_kernel,
        out_shape=jax.ShapeDtypeStruct((batch * seq, w), BF16),
        grid=(batch, nq),
        in_specs=[pl.BlockSpec((tq, w), lambda b, i: (b * nq + i, 0)),
                  pl.BlockSpec((mem_len, w), lambda b, i: (b, 0)),
                  pl.BlockSpec((mem_len, w), lambda b, i: (b, 1))],
        out_specs=pl.BlockSpec((tq, w), lambda b, i: (b * nq + i, 0)),
        compiler_params=_cparams(("parallel", "parallel")),
        name="mem_attn",
    )(qp, kv, kv)


def _sigmoid(x):
    return 1.0 / (1.0 + jnp.exp(-x))


def _merge_kernel(ya_ref, ys_ref, ym_ref, ga_ref, gs_ref, gm_ref, wa_ref, ws_ref, wm_ref, o_ref):
    acc = _sigmoid(ga_ref[...]) * jnp.dot(ya_ref[...], wa_ref[...], preferred_element_type=F32)
    acc += _sigmoid(gs_ref[...]) * jnp.dot(ys_ref[...], ws_ref[...], preferred_element_type=F32)
    acc += _sigmoid(gm_ref[...]) * jnp.dot(ym_ref[...], wm_ref[...], preferred_element_type=F32)
    o_ref[...] = acc.astype(o_ref.dtype)


def branch_merge(ya, ys, ym, proj, gate_col, wa, ws, wm, tm, tn, layer=None):
    m = ya.shape[0]
    d = wa.shape[-1]
    per = d // tn
    g0 = gate_col // tn
    yspec = lambda w: pl.BlockSpec((tm, w), lambda i, j: (i, 0))
    gspec = lambda k: pl.BlockSpec((tm, tn), lambda i, j: (i, g0 + k * per + j))
    wspec = lambda w: _wspec(w, layer, (w.shape[-2], tn), lambda i, j: (0, j))
    return pl.pallas_call(
        _merge_kernel,
        out_shape=jax.ShapeDtypeStruct((m, d), BF16),
        grid=(m // tm, per),
        in_specs=[yspec(ya.shape[1]), yspec(ys.shape[1]), yspec(ym.shape[1]),
                  gspec(0), gspec(1), gspec(2),
                  wspec(wa), wspec(ws), wspec(wm)],
        out_specs=pl.BlockSpec((tm, tn), lambda i, j: (i, j)),
        compiler_params=_cparams(("parallel", "arbitrary")),
        name="branch_merge",
    )(ya, ys, ym, proj, proj, proj, wa, ws, wm)


def _out_residual_kernel(a_ref, w_ref, x_ref, g_ref, o_ref):
    y = jnp.dot(a_ref[...], w_ref[...], preferred_element_type=F32)
    o_ref[...] = x_ref[...] + _rms(y, g_ref[...])


def out_residual(a, w, x, gain, tm, layer=None):
    m, d = x.shape
    return pl.pallas_call(
        _out_residual_kernel,
        out_shape=jax.ShapeDtypeStruct((m, d), F32),
        grid=(m // tm,),
        in_specs=[pl.BlockSpec((tm, a.shape[1]), lambda i: (i, 0)),
                  _wspec(w, layer, w.shape[-2:], lambda i: (0, 0)),
                  pl.BlockSpec((tm, d), lambda i: (i, 0)),
                  pl.BlockSpec((1, d), lambda i: (0, 0))],
        out_specs=pl.BlockSpec((tm, d), lambda i: (i, 0)),
        compiler_params=_cparams(("parallel",)),
        name="out_residual",
    )(a, w, x, gain.reshape(1, d))


def _gelu_tanh(x):
    return 0.5 * x * (1.0 + jnp.tanh(0.7978845608028654 * (x + 0.044715 * x * x * x)))


def _ffn_kernel(x_ref, xh_ref, gpre_ref, wa_ref, wg_ref, cwa_ref, cwg_ref, cba_ref, cbg_ref,
                wd_ref, gpost_ref, o_ref, hn_ref, u_ref, acc_ref, *, tm, seq, nf):
    i = pl.program_id(0)
    j = pl.program_id(1)
    seq_start = (i * tm) % seq == 0

    def up(a_ref, g_ref):
        for k, w_ref in enumerate((a_ref, g_ref)):
            u = jnp.dot(hn_ref[...], w_ref[...], preferred_element_type=F32)
            u_ref[k, 0:HALO, :] = jnp.where(seq_start, 0.0, u[0:HALO, :])
            u_ref[k, HALO:HALO + tm, :] = u[HALO:, :]

    def conv(k, cw_ref, cb_ref):
        out = cb_ref[...]
        for kk in range(FFN_CONV):
            off = HALO - (FFN_CONV - 1) + kk
            out = out + u_ref[k, pl.ds(off, tm), :] * cw_ref[kk:kk + 1, :]
        return out

    def consume():
        a = conv(0, cwa_ref, cba_ref)
        g = conv(1, cwg_ref, cbg_ref)
        act = (_gelu_tanh(a) * g).astype(BF16)
        acc_ref[...] += jnp.dot(act, wd_ref[...], preferred_element_type=F32)

    @pl.when(j == 0)
    def _():
        hn_ref[0:HALO, :] = _rms(xh_ref[...], gpre_ref[...]).astype(BF16)
        hn_ref[HALO:HALO + tm, :] = _rms(x_ref[...], gpre_ref[...]).astype(BF16)
        acc_ref[...] = jnp.zeros(acc_ref.shape, F32)

    up(wa_ref, wg_ref)
    consume()

    @pl.when(j == nf - 1)
    def _():
        o_ref[...] = x_ref[...] + _rms(acc_ref[...], gpost_ref[...])


def conv_glu_ffn(x, gpre, w_up, conv_w, conv_b, w_down, gpost, seq, tm, tf, layer=None):
    m, d = x.shape
    dff = w_down.shape[-2]
    nf = dff // tf
    hb = tm // HALO
    vec = lambda: pl.BlockSpec((1, d), lambda i, j: (0, 0))
    return pl.pallas_call(
        functools.partial(_ffn_kernel, tm=tm, seq=seq, nf=nf),
        out_shape=jax.ShapeDtypeStruct((m, d), F32),
        grid=(m // tm, nf),
        in_specs=[pl.BlockSpec((tm, d), lambda i, j: (i, 0)),
                  pl.BlockSpec((HALO, d), lambda i, j: (jnp.maximum(i * hb - 1, 0), 0)),
                  vec(),
                  _wspec(w_up, layer, (d, tf), lambda i, j: (0, j)),
                  _wspec(w_up, layer, (d, tf), lambda i, j: (0, nf + j)),
                  pl.BlockSpec((FFN_CONV, tf), lambda i, j: (0, j)),
                  pl.BlockSpec((FFN_CONV, tf), lambda i, j: (0, nf + j)),
                  pl.BlockSpec((1, tf), lambda i, j: (0, j)),
                  pl.BlockSpec((1, tf), lambda i, j: (0, nf + j)),
                  _wspec(w_down, layer, (tf, d), lambda i, j: (j, 0)),
                  vec()],
        out_specs=pl.BlockSpec((tm, d), lambda i, j: (i, 0)),
        scratch_shapes=[pltpu.VMEM((HALO + tm, d), BF16),
                        pltpu.VMEM((2, HALO + tm, tf), F32),
                        pltpu.VMEM((tm, d), F32)],
        compiler_params=_cparams(("parallel", "arbitrary")),
        name="conv_glu_ffn",
    )(x, x, gpre.reshape(1, d), w_up, w_up, conv_w, conv_w, conv_b.reshape(1, -1),
      conv_b.reshape(1, -1), w_down, gpost.reshape(1, d))


def _rope_tables(seq):
    half = ATT_HEAD_DIM // 2
    inv = ROPE_THETA ** (-jnp.arange(half, dtype=F32) / half)
    ang = jnp.arange(seq, dtype=F32)[:, None] * inv[None, :]
    cos, sin = jnp.cos(ang), jnp.sin(ang)
    return jnp.concatenate([cos, cos], axis=1), jnp.concatenate([-sin, sin], axis=1)


def kernel(x, mem, norm_mix_pre, norm_mix_post, norm_ffn_pre, norm_ffn_post, norm_mem, w_in,
           conv_ssd_w, conv_ssd_b, dt_bias, a_log, d_skip, ssd_norm, w_mem_kv, w_br_attn, w_br_ssd,
           w_br_mem, w_out, w_up, conv_ffn_w, conv_ffn_b, w_down):
    batch, seq, d = x.shape
    depth = w_in.shape[0]
    mem_len = mem.shape[1]
    ssd_w = w_br_ssd.shape[1]
    heads = ssd_w // SSD_HEAD_DIM
    bcw = 2 * SSD_GROUPS * SSD_STATE
    mem_w = MEM_HEADS * MEM_HEAD_DIM
    assert seq % MOBA_BLOCK == 0 and seq % SSD_CHUNK == 0 and d == ssd_w

    o_q, o_z = 0, 3 * ATT_WIDTH
    o_xs = o_z + ssd_w
    o_bc = o_xs + ssd_w
    o_dt = o_bc + bcw
    o_qm = o_dt + heads
    o_g = o_qm + mem_w

    half = ssd_w // 2
    assert o_z % half == 0 and o_xs % half == 0 and o_bc % bcw == 0 and o_q % ATT_WIDTH == 0

    cos, sin = _rope_tables(seq)
    xf = x.reshape(batch * seq, d)
    memf = mem.reshape(batch * mem_len, d)

    w_all = w_in.astype(BF16)
    w_gate = w_all[:, :, o_g:]
    w_b = jnp.concatenate([w_all[:, :, o_qm:o_g], w_all[:, :, o_dt:o_qm],
                           jnp.zeros((depth, d, LANES - heads), BF16)], axis=2)
    w_kv, w_ba, w_bs, w_bm, w_o, w_u, w_d = (
        t.astype(BF16) for t in (w_mem_kv, w_br_attn, w_br_ssd, w_br_mem, w_out, w_up, w_down))

    for l in range(depth):
        proj = norm_matmul(xf, norm_mix_pre[l], w_all, F32, tm=1024, tn=1024, layer=l, n=o_dt)
        gates = norm_matmul(xf, norm_mix_pre[l], w_gate, F32, tm=1024, tn=1024, layer=l)
        projb = norm_matmul(xf, norm_mix_pre[l], w_b, F32, tm=1024, tn=w_b.shape[2], layer=l)

        qa, ka, vt = rope_route(proj, cos, sin, o_q // ATT_WIDTH, seq)
        y_a = moba_attention(qa, ka, vt, batch, seq)
        y_s = ssd_mixer(proj, projb, o_z // half, o_xs // half, o_bc // bcw, mem_w // LANES,
                        conv_ssd_w[l], conv_ssd_b[l], dt_bias[l], a_log[l], d_skip[l], ssd_norm[l],
                        batch, seq, ssd_w)
        kv = norm_matmul(memf, norm_mem[l], w_kv, BF16, tm=batch * mem_len, tn=512, layer=l)
        y_m = mem_attention(projb, kv, batch, seq, tq=512)
        merged = branch_merge(y_a, y_s, y_m, gates, 0, w_ba, w_bs, w_bm, tm=1024, tn=512, layer=l)
        xf = out_residual(merged, w_o, xf, norm_mix_post[l], tm=512, layer=l)
        xf = conv_glu_ffn(xf, norm_ffn_pre[l], w_u, conv_ffn_w[l], conv_ffn_b[l], w_d,
                          norm_ffn_post[l], seq, tm=512, tf=512, layer=l)
    return xf.reshape(batch, seq, d)
```
